```python
import math
import jax, jax.numpy as jnp
from jax import lax
import numpy as np

D_MODEL = 1024
BATCH = 4
SEQ = 8192
DEPTH = 2

GLA_HEADS = 4
GLA_DK = 64
GLA_DV = 128
GLA_GATE_RANK = 16
GLA_GATE_TAU = 16.0
GLA_CHUNK = 64
ATT_HEADS = 8
ATT_HD = 64
DILATED_BRANCHES = ((128, 1), (512, 4), (2048, 16))
GLA_QK_W = GLA_HEADS * GLA_DK
GLA_V_W = GLA_HEADS * GLA_DV
ATT_W = ATT_HEADS * ATT_HD
AB_IN_W = 2 * GLA_QK_W + 2 * GLA_V_W + GLA_GATE_RANK + 3 * ATT_W
AB_OUT_W = GLA_V_W + ATT_W
D_RNN = 1024
RNN_BLOCKS = 8
RNN_BLOCK_W = D_RNN // RNN_BLOCKS
CONV_W = 4
LRU_C = 8.0
PEER_HEADS = 8
PEER_NKEYS = 128
PEER_EXPERTS = PEER_NKEYS * PEER_NKEYS
PEER_DQ = 256
PEER_TOPK = 16
PEER_BLOCK = 128
NORM_EPS = 1e-6
N_EVEN = (DEPTH + 1) // 2
N_ODD = DEPTH // 2

kernel_name = "hybrid_gla_dilated_rglru_peer"


def rms_norm(x, gain):
    xf = x.astype(jnp.float32)
    y = xf * lax.rsqrt(jnp.mean(xf * xf, axis=-1, keepdims=True) + NORM_EPS)
    return (y * gain.astype(jnp.float32)).astype(x.dtype)


def alibi_slopes(n):
    return 2.0 ** (-8.0 * jnp.arange(1, n + 1, dtype=jnp.float32) / n)


def gla_chunked(q, k, v, log_a):
    f32 = jnp.float32
    B, T, H, DK = q.shape
    DV = v.shape[-1]
    C = GLA_CHUNK
    N = T // C
    q = q.astype(f32).reshape(B, N, C, H, DK) * (DK ** -0.5)
    k = k.astype(f32).reshape(B, N, C, H, DK)
    v = v.astype(f32).reshape(B, N, C, H, DV)
    b = jnp.cumsum(log_a.astype(f32).reshape(B, N, C, H, DK), axis=2)
    b_last = b[:, :, -1]
    q_dec = q * jnp.exp(b)
    k_inv = k * jnp.exp(-b)
    causal = jnp.tril(jnp.ones((C, C), dtype=bool))
    att = jnp.where(causal, jnp.einsum('bnihk,bnjhk->bnhij', q_dec, k_inv), 0.0)
    o_intra = jnp.einsum('bnhij,bnjhv->bnihv', att, v)
    k_to_end = k * jnp.exp(b_last[:, :, None] - b)
    chunk_kv = jnp.einsum('bnjhk,bnjhv->bnhkv', k_to_end, v)

    def step(state, inp):
        decay, kv = inp
        return decay[..., None] * state + kv, state

    s0 = jnp.zeros((B, H, DK, DV), f32)
    _, s_prev = lax.scan(step, s0, (jnp.moveaxis(jnp.exp(b_last), 1, 0), jnp.moveaxis(chunk_kv, 1, 0)))
    s_prev = jnp.moveaxis(s_prev, 0, 1)
    o_inter = jnp.einsum('bnihk,bnhkv->bnihv', q_dec, s_prev)
    return (o_intra + o_inter).reshape(B, T, H, DV)


def strided_window_attention(q, k, v, n_sub, dilation, slopes):
    f32 = jnp.float32
    B, T, H, hd = q.shape
    L = T // dilation

    def to_sub(x):
        return x.reshape(B, L, dilation, H, hd).transpose(0, 2, 1, 3, 4).reshape(B * dilation, L, H, hd)

    def from_sub(x):
        rest = x.shape[2:]
        return x.reshape(B, dilation, L, *rest).swapaxes(1, 2).reshape(B, T, *rest)

    blk = n_sub
    nb = -(-L // blk)
    lp = nb * blk
    nn = B * dilation
    pad = ((0, 0), (0, lp - L), (0, 0), (0, 0))
    qs = jnp.pad(to_sub(q), pad)
    ks = jnp.pad(to_sub(k), pad)
    vs = jnp.pad(to_sub(v), pad)
    qb = qs.reshape(nn, nb, blk, H, hd)

    def with_prev(x):
        xp = jnp.pad(x, ((0, 0), (blk, 0), (0, 0), (0, 0)))
        prev = xp[:, :lp].reshape(nn, nb, blk, H, hd)
        cur = x.reshape(nn, nb, blk, H, hd)
        return jnp.concatenate([prev, cur], axis=2)

    kb, vb = with_prev(ks), with_prev(vs)
    s = jnp.einsum('nbqhd,nbkhd->nbhqk', qb, kb).astype(f32) * (hd ** -0.5)
    qi = jnp.arange(blk)[:, None]
    kj = jnp.arange(2 * blk)[None, :]
    delta = qi + blk - kj
    in_band = (delta >= 0) & (delta <= n_sub)
    not_front = (jnp.arange(nb)[:, None, None] > 0) | (kj >= blk)[None]
    valid = in_band[None] & not_front
    bias = -slopes[:, None, None] * (delta * dilation).astype(f32)
    s = jnp.where(valid[None, :, None], s + bias[None, None], -1e30)
    lse = jax.nn.logsumexp(s, axis=-1)
    p = jnp.exp(s - lse[..., None])
    o = jnp.einsum('nbhqk,nbkhd->nbqhd', p, vb.astype(f32)).reshape(nn, lp, H, hd)[:, :L]
    lse = jnp.moveaxis(lse, 2, 3).reshape(nn, lp, H)[:, :L]
    return from_sub(o), from_sub(lse)


def dilated_mixture_attention(q, k, v):
    slopes = alibi_slopes(ATT_HEADS)
    outs, lses = [], []
    for window, dil in DILATED_BRANCHES:
        o, l = strided_window_attention(q, k, v, window // dil, dil, slopes)
        outs.append(o)
        lses.append(l)
    w = jax.nn.softmax(jnp.stack(lses, axis=0), axis=0)
    return jnp.sum(w[..., None] * jnp.stack(outs, axis=0), axis=0)


def gla_dilated_mixer(h, w_in, w_alpha, b_alpha, head_gain, w_out):
    B, T, _ = h.shape
    f32 = jnp.float32
    proj = h @ w_in
    sizes = [GLA_QK_W, GLA_QK_W, GLA_V_W, GLA_V_W, GLA_GATE_RANK, ATT_W, ATT_W, ATT_W]
    splits = np.cumsum(sizes)[:-1].tolist()
    gq, gk, gv, gg, glr, aq, ak, av = jnp.split(proj, splits, axis=-1)
    log_a = jax.nn.log_sigmoid((glr @ w_alpha + b_alpha).astype(f32)) / GLA_GATE_TAU
    o_gla = gla_chunked(gq.reshape(B, T, GLA_HEADS, GLA_DK), gk.reshape(B, T, GLA_HEADS, GLA_DK),
                        gv.reshape(B, T, GLA_HEADS, GLA_DV), log_a.reshape(B, T, GLA_HEADS, GLA_DK))
    o_gla = o_gla * lax.rsqrt(jnp.mean(o_gla * o_gla, axis=-1, keepdims=True) + NORM_EPS)
    o_gla = o_gla.reshape(B, T, GLA_V_W) * head_gain.astype(f32) * jax.nn.silu(gg.astype(f32))
    o_att = dilated_mixture_attention(aq.reshape(B, T, ATT_HEADS, ATT_HD), ak.reshape(B, T, ATT_HEADS, ATT_HD),
                                      av.reshape(B, T, ATT_HEADS, ATT_HD)).reshape(B, T, ATT_W)
    mixed = jnp.concatenate([o_gla, o_att], axis=-1).astype(h.dtype)
    return mixed @ w_out


def rglru_mixer(h, w_in, conv_w, conv_b, w_rgate, b_rgate, w_igate, b_igate, lam, w_out):
    B, T, _ = h.shape
    f32 = jnp.float32
    proj = h @ w_in
    gate, xr = jnp.split(proj, 2, axis=-1)
    xc = lax.conv_general_dilated(xr, conv_w[:, None, :], window_strides=(1,), padding=[(CONV_W - 1, 0)],
                                  dimension_numbers=('NWC', 'WIO', 'NWC'), feature_group_count=D_RNN) + conv_b
    xb = xc.reshape(B, T, RNN_BLOCKS, RNN_BLOCK_W)
    r = jax.nn.sigmoid(jnp.einsum('btgi,gij->btgj', xb, w_rgate).reshape(B, T, D_RNN) + b_rgate)
    i = jax.nn.sigmoid(jnp.einsum('btgi,gij->btgj', xb, w_igate).reshape(B, T, D_RNN) + b_igate)
    log_a = -LRU_C * r.astype(f32) * jax.nn.softplus(-lam.astype(f32))
    a = jnp.exp(log_a)
    u = jnp.sqrt(-jnp.expm1(2.0 * log_a)) * (i * xc).astype(f32)

    def combine(left, right):
        a1, b1 = left
        a2, b2 = right
        return a1 * a2, a2 * b1 + b2

    _, hs = lax.associative_scan(combine, (a, u), axis=1)
    y = (jax.nn.gelu(gate.astype(f32)) * hs).astype(h.dtype)
    return y @ w_out


def peer_ffn(h, w_query, sub_keys, u_tab, v_tab):
    B, T, D = h.shape
    f32 = jnp.float32
    M = B * T
    K = PEER_TOPK
    H = PEER_HEADS
    hf = h.reshape(M, D)
    q = (hf @ w_query).reshape(M, H, 2, PEER_DQ // 2).astype(f32)
    s = jnp.einsum('mhpc,hpnc->mhpn', q, sub_keys.astype(f32))
    top_s, top_i = lax.top_k(s, K)
    cand_s = top_s[:, :, 0, :, None] + top_s[:, :, 1, None, :]
    cand_i = top_i[:, :, 0, :, None] * PEER_NKEYS + top_i[:, :, 1, None, :]
    best_s, best_c = lax.top_k(cand_s.reshape(M, H, K * K), K)
    ids = jnp.take_along_axis(cand_i.reshape(M, H, K * K), best_c, axis=-1)
    g = jax.nn.softmax(best_s, axis=-1)
    nblk = M // PEER_BLOCK
    ids = ids.reshape(nblk, PEER_BLOCK, H * K)
    g = g.reshape(nblk, PEER_BLOCK, H * K)
    hb = hf.reshape(nblk, PEER_BLOCK, D)

    def block(args):
        xb, idb, gb = args
        u = jnp.take(u_tab, idb, axis=0)
        v = jnp.take(v_tab, idb, axis=0)
        act = jax.nn.gelu(jnp.einsum('nkd,nd->nk', u, xb).astype(f32)) * gb
        return jnp.einsum('nk,nkd->nd', act.astype(v.dtype), v)

    out = lax.map(block, (hb, ids, g))
    return out.reshape(B, T, D).astype(h.dtype)


def setup_inputs(seed: int = 0) -> dict:
    key = jax.random.key(seed)
    ks = jax.random.split(key, 24)
    nrm = jax.random.normal
    D = D_MODEL
    lru_u = jax.random.uniform(ks[14], (N_ODD, D_RNN), minval=0.9, maxval=0.999)
    lru_a = lru_u ** (1.0 / LRU_C)
    return {
        "x": nrm(ks[0], (BATCH, SEQ, D), jnp.float32),
        "mix_norm": 1.0 + 0.02 * nrm(ks[1], (DEPTH, D)),
        "ffn_norm": 1.0 + 0.02 * nrm(ks[2], (DEPTH, D)),
        "ab_w_in": nrm(ks[3], (N_EVEN, D, AB_IN_W)) * D ** -0.5,
        "gla_w_alpha": nrm(ks[4], (N_EVEN, GLA_GATE_RANK, GLA_QK_W)) * GLA_GATE_RANK ** -0.5,
        "gla_b_alpha": 0.1 * nrm(ks[5], (N_EVEN, GLA_QK_W)),
        "gla_head_gain": 1.0 + 0.02 * nrm(ks[6], (N_EVEN, GLA_V_W)),
        "ab_w_out": nrm(ks[7], (N_EVEN, AB_OUT_W, D)) * AB_OUT_W ** -0.5,
        "c_w_in": nrm(ks[8], (N_ODD, D, 2 * D_RNN)) * D ** -0.5,
        "c_conv_w": nrm(ks[9], (N_ODD, CONV_W, D_RNN)) * CONV_W ** -0.5,
        "c_conv_b": 0.02 * nrm(ks[10], (N_ODD, D_RNN)),
        "c_w_rgate": nrm(ks[11], (N_ODD, RNN_BLOCKS, RNN_BLOCK_W, RNN_BLOCK_W)) * RNN_BLOCK_W ** -0.5,
        "c_b_rgate": 0.02 * nrm(ks[12], (N_ODD, D_RNN)),
        "c_w_igate": nrm(ks[13], (N_ODD, RNN_BLOCKS, RNN_BLOCK_W, RNN_BLOCK_W)) * RNN_BLOCK_W ** -0.5,
        "c_b_igate": 0.02 * nrm(ks[15], (N_ODD, D_RNN)),
        "c_lambda": jnp.log(lru_a) - jnp.log1p(-lru_a),
        "c_w_out": nrm(ks[16], (N_ODD, D_RNN, D)) * D_RNN ** -0.5,
        "peer_w_query": nrm(ks[17], (DEPTH, D, PEER_HEADS * PEER_DQ)) * D ** -0.5,
        "peer_sub_keys": nrm(ks[18], (DEPTH, PEER_HEADS, 2, PEER_NKEYS, PEER_DQ // 2)) * (PEER_DQ // 2) ** -0.5,
        "peer_u": nrm(ks[19], (DEPTH, PEER_EXPERTS, D)) * D ** -0.5,
        "peer_v": nrm(ks[20], (DEPTH, PEER_EXPERTS, D)) * (PEER_HEADS * PEER_TOPK) ** -0.5,
        "final_norm": 1.0 + 0.02 * nrm(ks[21], (D,)),
    }


def reference(x, mix_norm, ffn_norm, ab_w_in, gla_w_alpha, gla_b_alpha, gla_head_gain, ab_w_out,
              c_w_in, c_conv_w, c_conv_b, c_w_rgate, c_b_rgate, c_w_igate, c_b_igate, c_lambda, c_w_out,
              peer_w_query, peer_sub_keys, peer_u, peer_v, final_norm):
    h = x
    for layer in range(DEPTH):
        hn = rms_norm(h, mix_norm[layer])
        if layer % 2 == 0:
            e = layer // 2
            h = h + gla_dilated_mixer(hn, ab_w_in[e], gla_w_alpha[e], gla_b_alpha[e], gla_head_gain[e], ab_w_out[e])
        else:
            o = layer // 2
            h = h + rglru_mixer(hn, c_w_in[o], c_conv_w[o], c_conv_b[o], c_w_rgate[o], c_b_rgate[o],
                                c_w_igate[o], c_b_igate[o], c_lambda[o], c_w_out[o])
        hn = rms_norm(h, ffn_norm[layer])
        h = h + peer_ffn(hn, peer_w_query[layer], peer_sub_keys[layer], peer_u[layer], peer_v[layer])
    return rms_norm(h, final_norm)
```

```python
import functools
import math

import jax
import jax.numpy as jnp
from jax import lax
from jax.experimental import pallas as pl
from jax.experimental.pallas import tpu as pltpu

F32 = jnp.float32
BF16 = jnp.bfloat16

NORM_EPS = 1e-6
V7X_VMEM_LIMIT = 56 * 1024 * 1024

GLA_HEADS = 4
GLA_DK = 64
GLA_DV = 128
GLA_GATE_RANK = 16
GLA_GATE_TAU = 16.0
GLA_CHUNK = 64
ATT_HEADS = 8
ATT_HD = 64
DILATED_BRANCHES = ((128, 1), (512, 4), (2048, 16))
GLA_QK_W = GLA_HEADS * GLA_DK
GLA_V_W = GLA_HEADS * GLA_DV
ATT_W = ATT_HEADS * ATT_HD
CONV_W = 4
LRU_C = 8.0
RNN_BLOCK_W = 128
PEER_HEADS = 8
PEER_NKEYS = 128
PEER_TOPK = 16
PEER_SLOTS = PEER_HEADS * PEER_TOPK
LANES = 128

_NT = (((1,), (1,)), ((), ()))
_TN = (((0,), (0,)), ((), ()))


def _params(*sem):
    return pltpu.CompilerParams(dimension_semantics=sem, vmem_limit_bytes=V7X_VMEM_LIMIT)


def _gelu_tanh(x):
    return 0.5 * x * (1.0 + jnp.tanh(math.sqrt(2.0 / math.pi) * (x + 0.044715 * (x * x * x))))


def _rms(x, gain):
    ms = jnp.mean(x * x, axis=-1, keepdims=True)
    return x * lax.rsqrt(ms + NORM_EPS) * gain


def _norm_matmul_kernel(x_ref, g_ref, w_ref, *o_refs, widths):
    xn = _rms(x_ref[...], g_ref[...]).astype(BF16)
    off = 0
    for o_ref, w in zip(o_refs, widths):
        o_ref[...] = jnp.dot(xn, w_ref[:, off:off + w], preferred_element_type=F32).astype(o_ref.dtype)
        off += w


def norm_matmul(x, gain, w_bf16, widths, dtypes, tm=512):
    m, d = x.shape
    n = w_bf16.shape[1]
    assert sum(widths) == n and m % tm == 0
    return pl.pallas_call(
        functools.partial(_norm_matmul_kernel, widths=tuple(widths)),
        grid=(m // tm,),
        in_specs=[pl.BlockSpec((tm, d), lambda i: (i, 0)),
                  pl.BlockSpec((1, d), lambda i: (0, 0)),
                  pl.BlockSpec((d, n), lambda i: (0, 0))],
        out_specs=[pl.BlockSpec((tm, w), lambda i: (i, 0)) for w in widths],
        out_shape=[jax.ShapeDtypeStruct((m, w), dt) for w, dt in zip(widths, dtypes)],
        compiler_params=_params("parallel"),
        name="norm_matmul",
    )(x, gain.reshape(1, d), w_bf16)


def _matmul_res_kernel(x_ref, w_ref, r_ref, o_ref):
    o_ref[...] = r_ref[...] + jnp.dot(x_ref[...].astype(BF16), w_ref[...], preferred_element_type=F32)


def matmul_res(x, w_bf16, res, tm=512):
    m, k = x.shape
    n = w_bf16.shape[1]
    return pl.pallas_call(
        _matmul_res_kernel,
        grid=(m // tm,),
        in_specs=[pl.BlockSpec((tm, k), lambda i: (i, 0)),
                  pl.BlockSpec((k, n), lambda i: (0, 0)),
                  pl.BlockSpec((tm, n), lambda i: (i, 0))],
        out_specs=pl.BlockSpec((tm, n), lambda i: (i, 0)),
        out_shape=jax.ShapeDtypeStruct((m, n), F32),
        compiler_params=_params("parallel"),
        name="matmul_res",
    )(x, w_bf16, res)


def _gla_kernel(q_ref, k_ref, v_ref, gg_ref, lr_ref, wa_ref, ba_ref, hg_ref, o_ref, st_ref, *, tg):
    c = GLA_CHUNK

    @pl.when(pl.program_id(1) == 0)
    def _():
        st_ref[...] = jnp.zeros_like(st_ref)

    z = jnp.dot(lr_ref[0], wa_ref[...], preferred_element_type=F32,
                precision=lax.Precision.HIGHEST) + ba_ref[...]
    log_a = (jnp.minimum(z, 0.0) - jnp.log1p(jnp.exp(-jnp.abs(z)))) * (1.0 / GLA_GATE_TAU)
    row = lax.broadcasted_iota(jnp.int32, (c, c), 0)
    col = lax.broadcasted_iota(jnp.int32, (c, c), 1)
    causal = row >= col
    tril = causal.astype(F32)
    for ci in range(tg // c):
        sl = slice(ci * c, (ci + 1) * c)
        b = jnp.dot(tril, log_a[sl], preferred_element_type=F32, precision=lax.Precision.HIGHEST)
        b_last = b[c - 1:c, :]
        eb = jnp.exp(b)
        q_dec = q_ref[0, sl, :] * (GLA_DK ** -0.5) * eb
        kk = k_ref[0, sl, :]
        k_inv = kk * jnp.exp(-b)
        k_end = kk * jnp.exp(b_last - b)
        decay = jnp.exp(b_last)
        vv = v_ref[0, sl, :]
        outs = []
        for h in range(GLA_HEADS):
            ks = slice(h * GLA_DK, (h + 1) * GLA_DK)
            vs = slice(h * GLA_DV, (h + 1) * GLA_DV)
            qh = q_dec[:, ks].astype(BF16)
            vh = vv[:, vs].astype(BF16)
            att = lax.dot_general(qh, k_inv[:, ks].astype(BF16), _NT, preferred_element_type=F32)
            att = jnp.where(causal, att, 0.0)
            st = st_ref[h]
            o_h = jnp.dot(att.astype(BF16), vh, preferred_element_type=F32)
            o_h = o_h + lax.dot_general(qh, st.astype(BF16), _NT, preferred_element_type=F32)
            kv_t = lax.dot_general(vh, k_end[:, ks].astype(BF16), _TN, preferred_element_type=F32)
            st_ref[h] = decay[:, ks] * st + kv_t
            o_h = o_h * lax.rsqrt(jnp.mean(o_h * o_h, axis=-1, keepdims=True) + NORM_EPS)
            outs.append(o_h)
        o = jnp.concatenate(outs, axis=-1)
        gg = gg_ref[0, sl, :]
        o_ref[0, sl, :] = o * hg_ref[...] * (gg * jax.nn.sigmoid(gg))


def gla(gq, gk, gv, gg, glr, w_alpha, b_alpha, head_gain, tg=256):
    bsz, t, _ = gq.shape
    wa = jnp.zeros((LANES, GLA_QK_W), F32).at[:GLA_GATE_RANK].set(w_alpha)
    spec = lambda w: pl.BlockSpec((1, tg, w), lambda b, i: (b, i, 0))
    const = lambda r, w: pl.BlockSpec((r, w), lambda b, i: (0, 0))
    return pl.pallas_call(
        functools.partial(_gla_kernel, tg=tg),
        grid=(bsz, t // tg),
        in_specs=[spec(GLA_QK_W), spec(GLA_QK_W), spec(GLA_V_W), spec(GLA_V_W), spec(LANES),
                  const(LANES, GLA_QK_W), const(1, GLA_QK_W), const(1, GLA_V_W)],
        out_specs=spec(GLA_V_W),
        out_shape=jax.ShapeDtypeStruct((bsz, t, GLA_V_W), F32),
        scratch_shapes=[pltpu.VMEM((GLA_HEADS, GLA_DV, GLA_DK), F32)],
        compiler_params=_params("parallel", "arbitrary"),
        name="gla",
    )(gq, gk, gv, gg, glr, wa, b_alpha.reshape(1, -1), head_gain.reshape(1, -1))


def _dilated_kernel(q_ref, kp_ref, kc_ref, vp_ref, vc_ref, o_ref, l_ref, *, dil, blk):
    i = pl.program_id(2)
    qi = lax.broadcasted_iota(jnp.int32, (blk, 2 * blk), 0)
    kj = lax.broadcasted_iota(jnp.int32, (blk, 2 * blk), 1)
    delta = qi + blk - kj
    kj_min = jnp.where(i > 0, 0, blk)
    valid = (delta >= 0) & (delta <= blk) & (kj >= kj_min)
    dist = (delta * dil).astype(F32)
    lane = lax.broadcasted_iota(jnp.int32, (1, 2 * ATT_HD), 1)
    lo = lane < ATT_HD
    for p in range(ATT_HEADS // 2):
        cs = slice(p * 2 * ATT_HD, (p + 1) * 2 * ATT_HD)
        q2 = q_ref[0, :, cs]
        k2 = jnp.concatenate([kp_ref[0, :, cs], kc_ref[0, :, cs]], axis=0)
        v2 = jnp.concatenate([vp_ref[0, :, cs], vc_ref[0, :, cs]], axis=0)
        o_pair = jnp.zeros((blk, 2 * ATT_HD), F32)
        lse_pair = jnp.zeros((blk, 2 * ATT_HD), F32)
        for e in range(2):
            sel = lo if e == 0 else jnp.logical_not(lo)
            slope = 2.0 ** (-8.0 * (2 * p + e + 1) / ATT_HEADS)
            qm = jnp.where(sel, q2, jnp.zeros_like(q2))
            vm = jnp.where(sel, v2, jnp.zeros_like(v2))
            s = lax.dot_general(qm, k2, _NT, preferred_element_type=F32) * (ATT_HD ** -0.5)
            s = jnp.where(valid, s - slope * dist, -1e30)
            m = jnp.max(s, axis=-1, keepdims=True)
            pexp = jnp.exp(s - m)
            l = jnp.sum(pexp, axis=-1, keepdims=True)
            o_e = jnp.dot(pexp.astype(BF16), vm, preferred_element_type=F32)
            o_pair = o_pair + o_e * (1.0 / l)
            lse_pair = jnp.where(sel, m + jnp.log(l), lse_pair)
        o_ref[0, :, cs] = o_pair
        l_ref[0, :, cs] = lse_pair


def dilated_branch(q, k, v, window, dil):
    bsz, t, w = q.shape
    blk = window // dil
    l = t // dil
    assert l % blk == 0
    nb = l // blk
    qv, kv, vv = (a.reshape(bsz, l, dil * w) for a in (q, k, v))
    cur = pl.BlockSpec((1, blk, w), lambda b, r, i: (b, i, r))
    prev = pl.BlockSpec((1, blk, w), lambda b, r, i: (b, jnp.maximum(i - 1, 0), r))
    o, lse = pl.pallas_call(
        functools.partial(_dilated_kernel, dil=dil, blk=blk),
        grid=(bsz, dil, nb),
        in_specs=[cur, prev, cur, prev, cur],
        out_specs=[cur, cur],
        out_shape=[jax.ShapeDtypeStruct((bsz, l, dil * w), F32)] * 2,
        compiler_params=_params("parallel", "parallel", "arbitrary"),
        name=f"dilated_d{dil}",
    )(qv, kv, kv, vv, vv)
    return o.reshape(bsz * t, w), lse.reshape(bsz * t, w)


def _combine_out_kernel(og_ref, o1_ref, o2_ref, o3_ref, l1_ref, l2_ref, l3_ref, w_ref, r_ref, out_ref):
    l1, l2, l3 = l1_ref[...], l2_ref[...], l3_ref[...]
    mx = jnp.maximum(jnp.maximum(l1, l2), l3)
    e1, e2, e3 = jnp.exp(l1 - mx), jnp.exp(l2 - mx), jnp.exp(l3 - mx)
    att = (e1 * o1_ref[...] + e2 * o2_ref[...] + e3 * o3_ref[...]) / (e1 + e2 + e3)
    kg = og_ref.shape[1]
    acc = jnp.dot(og_ref[...].astype(BF16), w_ref[:kg, :], preferred_element_type=F32)
    acc = acc + jnp.dot(att.astype(BF16), w_ref[kg:, :], preferred_element_type=F32)
    out_ref[...] = r_ref[...] + acc


def combine_out(o_gla, outs, lses, w_bf16, res, tm=512):
    m = res.shape[0]
    n = w_bf16.shape[1]
    row = lambda w: pl.BlockSpec((tm, w), lambda i: (i, 0))
    return pl.pallas_call(
        _combine_out_kernel,
        grid=(m // tm,),
        in_specs=[row(GLA_V_W)] + [row(ATT_W)] * 6
                 + [pl.BlockSpec(w_bf16.shape, lambda i: (0, 0)), row(n)],
        out_specs=row(n),
        out_shape=jax.ShapeDtypeStruct((m, n), F32),
        compiler_params=_params("parallel"),
        name="combine_out",
    )(o_gla, *outs, *lses, w_bf16, res)


def _lru_kernel(gate_ref, xr_ref, cw_ref, cb_ref, wr_ref, br_ref, wi_ref, bi_ref, lam_ref, y_ref,
                tail_ref, h_ref, *, tt):
    @pl.when(pl.program_id(1) == 0)
    def _():
        tail_ref[...] = jnp.zeros_like(tail_ref)
        h_ref[...] = jnp.zeros_like(h_ref)

    xr = xr_ref[0]
    d = xr.shape[1]
    ext = jnp.concatenate([tail_ref[...], xr], axis=0)
    tail_ref[...] = xr[tt - 8:, :]
    xc = cb_ref[...] + jnp.zeros_like(xr)
    for j in range(CONV_W):
        off = 8 - (CONV_W - 1) + j
        xc = xc + cw_ref[j:j + 1, :] * ext[off:off + tt, :]
    xcb = xc.astype(BF16)
    rs, is_ = [], []
    for g in range(d // RNN_BLOCK_W):
        cs = slice(g * RNN_BLOCK_W, (g + 1) * RNN_BLOCK_W)
        rs.append(jnp.dot(xcb[:, cs], wr_ref[g], preferred_element_type=F32))
        is_.append(jnp.dot(xcb[:, cs], wi_ref[g], preferred_element_type=F32))
    r = jax.nn.sigmoid(jnp.concatenate(rs, axis=-1) + br_ref[...])
    ig = jax.nn.sigmoid(jnp.concatenate(is_, axis=-1) + bi_ref[...])
    lam = lam_ref[...]
    softplus = jnp.maximum(-lam, 0.0) + jnp.log1p(jnp.exp(-jnp.abs(lam)))
    log_a = (-LRU_C) * r * softplus
    a = jnp.exp(log_a)
    th = jnp.tanh(log_a)
    u = jnp.sqrt(-2.0 * th / (1.0 - th)) * (ig * xc)
    rows = lax.broadcasted_iota(jnp.int32, (tt, 1), 0)
    s = 1
    while s < tt:
        keep = rows >= s
        a_sh = jnp.where(keep, pltpu.roll(a, s, axis=0), 1.0)
        u_sh = jnp.where(keep, pltpu.roll(u, s, axis=0), 0.0)
        u = u + a * u_sh
        a = a * a_sh
        s *= 2
    hs = u + a * h_ref[...]
    h_ref[...] = hs[tt - 1:tt, :]
    y_ref[0] = _gelu_tanh(gate_ref[0]) * hs


def lru(gate, xr, conv_w, conv_b, w_r, b_r, w_i, b_i, lam, tt=256):
    bsz, t, d = xr.shape
    spec = pl.BlockSpec((1, tt, d), lambda b, i: (b, i, 0))
    row = pl.BlockSpec((1, d), lambda b, i: (0, 0))
    wspec = pl.BlockSpec(w_r.shape, lambda b, i: (0, 0, 0))
    return pl.pallas_call(
        functools.partial(_lru_kernel, tt=tt),
        grid=(bsz, t // tt),
        in_specs=[spec, spec, pl.BlockSpec((CONV_W, d), lambda b, i: (0, 0)), row,
                  wspec, row, wspec, row, row],
        out_specs=spec,
        out_shape=jax.ShapeDtypeStruct((bsz, t, d), F32),
        scratch_shapes=[pltpu.VMEM((8, d), F32), pltpu.VMEM((1, d), F32)],
        compiler_params=_params("parallel", "arbitrary"),
        name="lru",
    )(gate, xr, conv_w, conv_b.reshape(1, d), w_r.astype(BF16), b_r.reshape(1, d),
      w_i.astype(BF16), b_i.reshape(1, d), lam.reshape(1, d))


def _extract_top(vals, pos, n_rounds, payload=None):
    out_v, out_p, out_x = [], [], []
    for _ in range(n_rounds):
        m = jnp.max(vals, axis=0, keepdims=True)
        p = jnp.min(jnp.where(vals == m, pos, 1e9), axis=0, keepdims=True)
        first = pos == p
        if payload is not None:
            out_x.append(jnp.max(jnp.where(first, payload, -1.0), axis=0, keepdims=True))
        vals = jnp.where(first, -jnp.inf, vals)
        out_v.append(m)
        out_p.append(p)
    return out_v, out_p, out_x


def _peer_topk_kernel(q_ref, keys_ref, ids_ref, g_ref):
    k = PEER_TOPK
    tm = q_ref.shape[0]
    rows = lambda n: lax.broadcasted_iota(jnp.int32, (n, tm), 0).astype(F32)
    key_pos = rows(PEER_NKEYS)
    groups = [(0, 16)] + [(a, 8) for a in range(1, 8)]
    id_rows, g_rows = [], []
    for h in range(PEER_HEADS):
        tops = []
        for p in range(2):
            qs = q_ref[:, (2 * h + p) * LANES:(2 * h + p + 1) * LANES]
            s_t = lax.dot_general(keys_ref[h, p], qs, _NT, preferred_element_type=F32,
                                  precision=lax.Precision.HIGHEST)
            v, pidx, _ = _extract_top(s_t, key_pos, k)
            tops.append((jnp.concatenate(v, axis=0), jnp.concatenate(pidx, axis=0)))
        (t1, i1), (t2, i2) = tops
        cv, ci, cp = [], [], []
        for a, nb in groups:
            cv.append(t1[a:a + 1, :] + t2[0:nb, :])
            ci.append(i1[a:a + 1, :] * PEER_NKEYS + i2[0:nb, :])
            cp.append(a * k + rows(nb))
        cv.append(t1[8:16, :] + t2[0:1, :])
        ci.append(i1[8:16, :] * PEER_NKEYS + i2[0:1, :])
        cp.append((8.0 + rows(8)) * k)
        cand_v = jnp.concatenate(cv, axis=0)
        cand_i = jnp.concatenate(ci, axis=0)
        cand_p = jnp.concatenate(cp, axis=0)
        bv, _, bi = _extract_top(cand_v, cand_p, k, payload=cand_i)
        best = jnp.concatenate(bv, axis=0)
        e = jnp.exp(best - best[0:1, :])
        g_rows.append(e / jnp.sum(e, axis=0, keepdims=True))
        id_rows.append(jnp.concatenate(bi, axis=0))
    ids_t = jnp.concatenate(id_rows, axis=0)
    g_t = jnp.concatenate(g_rows, axis=0)
    for c in range(tm // LANES):
        cs = slice(c * LANES, (c + 1) * LANES)
        ids_ref[cs, :] = ids_t[:, cs].T.astype(jnp.int32)
        g_ref[cs, :] = g_t[:, cs].T


def peer_topk(q, sub_keys, tm=256):
    m = q.shape[0]
    return pl.pallas_call(
        _peer_topk_kernel,
        grid=(m // tm,),
        in_specs=[pl.BlockSpec((tm, q.shape[1]), lambda i: (i, 0)),
                  pl.BlockSpec(sub_keys.shape, lambda i: (0, 0, 0, 0))],
        out_specs=[pl.BlockSpec((tm, PEER_SLOTS), lambda i: (i, 0))] * 2,
        out_shape=[jax.ShapeDtypeStruct((m, PEER_SLOTS), jnp.int32),
                   jax.ShapeDtypeStruct((m, PEER_SLOTS), F32)],
        compiler_params=_params("parallel"),
        name="peer_topk",
    )(q, sub_keys)


def _peer_onehot_kernel(ids_ref, g_ref, w_ref):
    tm = ids_ref.shape[0]
    sub = lax.broadcasted_iota(jnp.int32, (PEER_NKEYS, PEER_SLOTS), 0)

    def body(m, carry):
        ids = ids_ref[pl.ds(m, 1), :]
        g = g_ref[pl.ds(m, 1), :]
        g_hi = g.astype(BF16).astype(F32)
        g_lo = g - g_hi
        hit1 = sub == (ids >> 7)
        hit2 = sub == (ids & (PEER_NKEYS - 1))
        c = jnp.concatenate([jnp.where(hit1, g_hi, 0.0), jnp.where(hit1, g_lo, 0.0)], axis=1)
        one = jnp.where(hit2, 1.0, 0.0)
        bt = jnp.concatenate([one, one], axis=1)
        w_ref[m] = lax.dot_general(c.astype(BF16), bt.astype(BF16), _NT, preferred_element_type=F32)
        return carry

    lax.fori_loop(0, tm, body, 0)


def peer_onehot(ids, g, tm=128):
    m = ids.shape[0]
    return pl.pallas_call(
        _peer_onehot_kernel,
        grid=(m // tm,),
        in_specs=[pl.BlockSpec((tm, PEER_SLOTS), lambda i: (i, 0))] * 2,
        out_specs=pl.BlockSpec((tm, PEER_NKEYS, PEER_NKEYS), lambda i: (i, 0, 0)),
        out_shape=jax.ShapeDtypeStruct((m, PEER_NKEYS, PEER_NKEYS), F32),
        compiler_params=_params("parallel"),
        name="peer_onehot",
    )(ids, g)


def _peer_dense_kernel(h_ref, gain_ref, w_ref, u_ref, v_ref, fg_ref, o_ref, hn_ref, acc_ref, *,
                       groups, final_norm):
    j = pl.program_id(1)

    @pl.when(j == 0)
    def _():
        hn_ref[...] = _rms(h_ref[...], gain_ref[...]).astype(BF16)
        acc_ref[...] = jnp.zeros_like(acc_ref)

    z = lax.dot_general(hn_ref[...], u_ref[...], _NT, preferred_element_type=F32)
    wsel = jnp.concatenate([w_ref[:, gi, :] for gi in range(groups)], axis=-1)
    act = (_gelu_tanh(z) * wsel).astype(BF16)
    acc_ref[...] += jnp.dot(act, v_ref[...], preferred_element_type=F32)

    @pl.when(j == pl.num_programs(1) - 1)
    def _():
        out = h_ref[...] + acc_ref[...]
        if final_norm:
            out = _rms(out, fg_ref[...])
        o_ref[...] = out


def peer_dense(h, gain, wmap, u_bf16, v_bf16, final_gain, final_norm, tm=512, groups=8):
    m, d = h.shape
    ne = u_bf16.shape[0]
    te = groups * PEER_NKEYS
    return pl.pallas_call(
        functools.partial(_peer_dense_kernel, groups=groups, final_norm=final_norm),
        grid=(m // tm, ne // te),
        in_specs=[pl.BlockSpec((tm, d), lambda i, j: (i, 0)),
                  pl.BlockSpec((1, d), lambda i, j: (0, 0)),
                  pl.BlockSpec((tm, groups, PEER_NKEYS), lambda i, j: (i, j, 0)),
                  pl.BlockSpec((te, d), lambda i, j: (j, 0)),
                  pl.BlockSpec((te, d), lambda i, j: (j, 0)),
                  pl.BlockSpec((1, d), lambda i, j: (0, 0))],
        out_specs=pl.BlockSpec((tm, d), lambda i, j: (i, 0)),
        out_shape=jax.ShapeDtypeStruct((m, d), F32),
        scratch_shapes=[pltpu.VMEM((tm, d), BF16), pltpu.VMEM((tm, d), F32)],
        compiler_params=_params("parallel", "arbitrary"),
        name="peer_dense",
    )(h, gain.reshape(1, d), wmap, u_bf16, v_bf16, final_gain.reshape(1, d))


def peer_ffn(h, gain, w_query, sub_keys, u_tab, v_tab, final_gain, final_norm):
    (q,) = norm_matmul(h, gain, w_query.astype(BF16), [w_query.shape[1]], [F32])
    ids, g = peer_topk(q, sub_keys)
    wmap = peer_onehot(ids, g)
    return peer_dense(h, gain, wmap, u_tab.astype(BF16), v_tab.astype(BF16), final_gain, final_norm)


def gla_dilated_layer(h, bsz, gain, w_in, w_alpha, b_alpha, head_gain, w_out):
    m, d = h.shape
    t = m // bsz
    sizes = [GLA_QK_W, GLA_QK_W, GLA_V_W, GLA_V_W, GLA_GATE_RANK, ATT_W, ATT_W, ATT_W]
    offs = [0]
    for s in sizes:
        offs.append(offs[-1] + s)
    seg = lambda i: w_in[:, offs[i]:offs[i + 1]]
    pad = jnp.zeros((d, LANES - GLA_GATE_RANK), w_in.dtype)
    w_cat = jnp.concatenate([seg(0), seg(1), seg(2), seg(3), seg(5), seg(6), seg(7), seg(4), pad], axis=1)
    widths = [GLA_QK_W, GLA_QK_W, GLA_V_W, GLA_V_W, ATT_W, ATT_W, ATT_W, LANES]
    dts = [F32, F32, F32, F32, BF16, BF16, BF16, F32]
    gq, gk, gv, gg, aq, ak, av, glr = norm_matmul(h, gain, w_cat.astype(BF16), widths, dts)
    r3 = lambda a: a.reshape(bsz, t, a.shape[1])
    o_gla = gla(r3(gq), r3(gk), r3(gv), r3(gg), r3(glr), w_alpha, b_alpha, head_gain)
    outs, lses = [], []
    for window, dil in DILATED_BRANCHES:
        o, l = dilated_branch(r3(aq), r3(ak), r3(av), window, dil)
        outs.append(o)
        lses.append(l)
    return combine_out(o_gla.reshape(m, GLA_V_W), outs, lses, w_out.astype(BF16), h)


def rglru_layer(h, bsz, gain, w_in, conv_w, conv_b, w_r, b_r, w_i, b_i, lam, w_out):
    m, d = h.shape
    t = m // bsz
    d_rnn = w_in.shape[1] // 2
    gate, xr = norm_matmul(h, gain, w_in.astype(BF16), [d_rnn, d_rnn], [F32, F32])
    y = lru(gate.reshape(bsz, t, d_rnn), xr.reshape(bsz, t, d_rnn), conv_w, conv_b, w_r, b_r, w_i, b_i, lam)
    return matmul_res(y.reshape(m, d_rnn), w_out.astype(BF16), h)


def kernel(x, mix_norm, ffn_norm, ab_w_in, gla_w_alpha, gla_b_alpha, gla_head_gain, ab_w_out,
           c_w_in, c_conv_w, c_conv_b, c_w_rgate, c_b_rgate, c_w_igate, c_b_igate, c_lambda, c_w_out,
           peer_w_query, peer_sub_keys, peer_u, peer_v, final_norm):
    bsz, t, d = x.shape
    depth = mix_norm.shape[0]
    h = x.reshape(bsz * t, d)
    for layer in range(depth):
        if layer % 2 == 0:
            e = layer // 2
            h = gla_dilated_layer(h, bsz, mix_norm[layer], ab_w_in[e], gla_w_alpha[e], gla_b_alpha[e],
                                  gla_head_gain[e], ab_w_out[e])
        else:
            o = layer // 2
            h = rglru_layer(h, bsz, mix_norm[layer], c_w_in[o], c_conv_w[o], c_conv_b[o], c_w_rgate[o],
                            c_b_rgate[o], c_w_igate[o], c_b_igate[o], c_lambda[o], c_w_out[o])
        h = peer_ffn(h, ffn_norm[layer], peer_w_query[layer], peer_sub_keys[layer], peer_u[layer],
                     peer_v[layer], final_norm, final_norm=(layer == depth - 1))
    return h.reshape(bsz, t, d)
```

```python
import functools
import math

import jax
import jax.numpy as jnp
from jax import lax
from jax.experimental import pallas as pl
from jax.experimental.pallas import tpu as pltpu

F32 = jnp.float32
BF16 = jnp.bfloat16

NORM_EPS = 1e-6
V7X_VMEM_LIMIT = 56 * 1024 * 1024

GLA_HEADS = 4
GLA_DK = 64
GLA_DV = 128
GLA_GATE_RANK = 16
GLA_GATE_TAU = 16.0
GLA_CHUNK = 64
ATT_HEADS = 8
ATT_HD = 64
DILATED_BRANCHES = ((128, 1), (512, 4), (2048, 16))
GLA_QK_W = GLA_HEADS * GLA_DK
GLA_V_W = GLA_HEADS * GLA_DV
ATT_W = ATT_HEADS * ATT_HD
CONV_W = 4
LRU_C = 8.0
RNN_BLOCK_W = 128
PEER_HEADS = 8
PEER_NKEYS = 128
PEER_TOPK = 16
PEER_SLOTS = PEER_HEADS * PEER_TOPK
LANES = 128

_NT = (((1,), (1,)), ((), ()))
_TN = (((0,), (0,)), ((), ()))


def _params(*sem):
    return pltpu.CompilerParams(dimension_semantics=sem, vmem_limit_bytes=V7X_VMEM_LIMIT)


def _gelu_tanh(x):
    return 0.5 * x * (1.0 + jnp.tanh(math.sqrt(2.0 / math.pi) * (x + 0.044715 * (x * x * x))))


def _rms(x, gain):
    ms = jnp.mean(x * x, axis=-1, keepdims=True)
    return x * lax.rsqrt(ms + NORM_EPS) * gain


def _norm_matmul_kernel(x_ref, g_ref, w_ref, *o_refs, widths):
    xn = _rms(x_ref[...], g_ref[...]).astype(BF16)
    off = 0
    for o_ref, w in zip(o_refs, widths):
        o_ref[...] = jnp.dot(xn, w_ref[:, off:off + w], preferred_element_type=F32).astype(o_ref.dtype)
        off += w


def norm_matmul(x, gain, w_bf16, widths, dtypes, tm=512):
    m, d = x.shape
    n = w_bf16.shape[1]
    assert sum(widths) == n and m % tm == 0
    return pl.pallas_call(
        functools.partial(_norm_matmul_kernel, widths=tuple(widths)),
        grid=(m // tm,),
        in_specs=[pl.BlockSpec((tm, d), lambda i: (i, 0)),
                  pl.BlockSpec((1, d), lambda i: (0, 0)),
                  pl.BlockSpec((d, n), lambda i: (0, 0))],
        out_specs=[pl.BlockSpec((tm, w), lambda i: (i, 0)) for w in widths],
        out_shape=[jax.ShapeDtypeStruct((m, w), dt) for w, dt in zip(widths, dtypes)],
        compiler_params=_params("parallel"),
        name="norm_matmul",
    )(x, gain.reshape(1, d), w_bf16)


def _matmul_res_kernel(x_ref, w_ref, r_ref, o_ref):
    o_ref[...] = r_ref[...] + jnp.dot(x_ref[...].astype(BF16), w_ref[...], preferred_element_type=F32)


def matmul_res(x, w_bf16, res, tm=512):
    m, k = x.shape
    n = w_bf16.shape[1]
    return pl.pallas_call(
        _matmul_res_kernel,
        grid=(m // tm,),
        in_specs=[pl.BlockSpec((tm, k), lambda i: (i, 0)),
                  pl.BlockSpec((k, n), lambda i: (0, 0)),
                  pl.BlockSpec((tm, n), lambda i: (i, 0))],
        out_specs=pl.BlockSpec((tm, n), lambda i: (i, 0)),
        out_shape=jax.ShapeDtypeStruct((m, n), F32),
        compiler_params=_params("parallel"),
        name="matmul_res",
    )(x, w_bf16, res)


def _gla_kernel(q_ref, k_ref, v_ref, gg_ref, lr_ref, wa_ref, ba_ref, hg_ref, o_ref, st_ref, *, tg):
    c = GLA_CHUNK

    @pl.when(pl.program_id(1) == 0)
    def _():
        st_ref[...] = jnp.zeros_like(st_ref)

    z = jnp.dot(lr_ref[0], wa_ref[...], preferred_element_type=F32,
                precision=lax.Precision.HIGHEST) + ba_ref[...]
    log_a = (jnp.minimum(z, 0.0) - jnp.log1p(jnp.exp(-jnp.abs(z)))) * (1.0 / GLA_GATE_TAU)
    row = lax.broadcasted_iota(jnp.int32, (c, c), 0)
    col = lax.broadcasted_iota(jnp.int32, (c, c), 1)
    causal = row >= col
    tril = causal.astype(F32)
    for ci in range(tg // c):
        sl = slice(ci * c, (ci + 1) * c)
        b = jnp.dot(tril, log_a[sl], preferred_element_type=F32, precision=lax.Precision.HIGHEST)
        b_last = b[c - 1:c, :]
        eb = jnp.exp(b)
        q_dec = q_ref[0, sl, :] * (GLA_DK ** -0.5) * eb
        kk = k_ref[0, sl, :]
        k_inv = kk * jnp.exp(-b)
        k_end = kk * jnp.exp(b_last - b)
        decay = jnp.exp(b_last)
        vv = v_ref[0, sl, :]
        outs = []
        for h in range(GLA_HEADS):
            ks = slice(h * GLA_DK, (h + 1) * GLA_DK)
            vs = slice(h * GLA_DV, (h + 1) * GLA_DV)
            qh = q_dec[:, ks].astype(BF16)
            vh = vv[:, vs].astype(BF16)
            att = lax.dot_general(qh, k_inv[:, ks].astype(BF16), _NT, preferred_element_type=F32)
            att = jnp.where(causal, att, 0.0)
            st = st_ref[h]
            o_h = jnp.dot(att.astype(BF16), vh, preferred_element_type=F32)
            o_h = o_h + lax.dot_general(qh, st.astype(BF16), _NT, preferred_element_type=F32)
            kv_t = lax.dot_general(vh, k_end[:, ks].astype(BF16), _TN, preferred_element_type=F32)
            st_ref[h] = decay[:, ks] * st + kv_t
            o_h = o_h * lax.rsqrt(jnp.mean(o_h * o_h, axis=-1, keepdims=True) + NORM_EPS)
            outs.append(o_h)
        o = jnp.concatenate(outs, axis=-1)
        gg = gg_ref[0, sl, :]
        o_ref[0, sl, :] = o * hg_ref[...] * (gg * jax.nn.sigmoid(gg))


def gla(gq, gk, gv, gg, glr, w_alpha, b_alpha, head_gain, tg=256):
    bsz, t, _ = gq.shape
    wa = jnp.zeros((LANES, GLA_QK_W), F32).at[:GLA_GATE_RANK].set(w_alpha)
    spec = lambda w: pl.BlockSpec((1, tg, w), lambda b, i: (b, i, 0))
    const = lambda r, w: pl.BlockSpec((r, w), lambda b, i: (0, 0))
    return pl.pallas_call(
        functools.partial(_gla_kernel, tg=tg),
        grid=(bsz, t // tg),
        in_specs=[spec(GLA_QK_W), spec(GLA_QK_W), spec(GLA_V_W), spec(GLA_V_W), spec(LANES),
                  const(LANES, GLA_QK_W), const(1, GLA_QK_W), const(1, GLA_V_W)],
        out_specs=spec(GLA_V_W),
        out_shape=jax.ShapeDtypeStruct((bsz, t, GLA_V_W), F32),
        scratch_shapes=[pltpu.VMEM((GLA_HEADS, GLA_DV, GLA_DK), F32)],
        compiler_params=_params("parallel", "arbitrary"),
        name="gla",
    )(gq, gk, gv, gg, glr, wa, b_alpha.reshape(1, -1), head_gain.reshape(1, -1))


def _dilated_kernel(q_ref, kp_ref, kc_ref, vp_ref, vc_ref, o_ref, l_ref, *, dil, blk):
    i = pl.program_id(2)
    qi = lax.broadcasted_iota(jnp.int32, (blk, 2 * blk), 0)
    kj = lax.broadcasted_iota(jnp.int32, (blk, 2 * blk), 1)
    delta = qi + blk - kj
    kj_min = jnp.where(i > 0, 0, blk)
    valid = (delta >= 0) & (delta <= blk) & (kj >= kj_min)
    dist = (delta * dil).astype(F32)
    lane = lax.broadcasted_iota(jnp.int32, (1, 2 * ATT_HD), 1)
    lo = lane < ATT_HD
    for p in range(ATT_HEADS // 2):
        cs = slice(p * 2 * ATT_HD, (p + 1) * 2 * ATT_HD)
        q2 = q_ref[0, :, cs]
        k2 = jnp.concatenate([kp_ref[0, :, cs], kc_ref[0, :, cs]], axis=0)
        v2 = jnp.concatenate([vp_ref[0, :, cs], vc_ref[0, :, cs]], axis=0)
        o_pair = jnp.zeros((blk, 2 * ATT_HD), F32)
        lse_pair = jnp.zeros((blk, 2 * ATT_HD), F32)
        for e in range(2):
            sel = lo if e == 0 else jnp.logical_not(lo)
            slope = 2.0 ** (-8.0 * (2 * p + e + 1) / ATT_HEADS)
            qm = jnp.where(sel, q2, jnp.zeros_like(q2))
            vm = jnp.where(sel, v2, jnp.zeros_like(v2))
            s = lax.dot_general(qm, k2, _NT, preferred_element_type=F32) * (ATT_HD ** -0.5)
            s = jnp.where(valid, s - slope * dist, -1e30)
            m = jnp.max(s, axis=-1, keepdims=True)
            pexp = jnp.exp(s - m)
            l = jnp.sum(pexp, axis=-1, keepdims=True)
            o_e = jnp.dot(pexp.astype(BF16), vm, preferred_element_type=F32)
            o_pair = o_pair + o_e * (1.0 / l)
            lse_pair = jnp.where(sel, m + jnp.log(l), lse_pair)
        o_ref[0, :, cs] = o_pair
        l_ref[0, :, cs] = lse_pair


def dilated_branch(q, k, v, window, dil):
    bsz, t, w = q.shape
    blk = window // dil
    l = t // dil
    assert l % blk == 0
    nb = l // blk
    qv, kv, vv = (a.reshape(bsz, l, dil * w) for a in (q, k, v))
    cur = pl.BlockSpec((1, blk, w), lambda b, r, i: (b, i, r))
    prev = pl.BlockSpec((1, blk, w), lambda b, r, i: (b, jnp.maximum(i - 1, 0), r))
    o, lse = pl.pallas_call(
        functools.partial(_dilated_kernel, dil=dil, blk=blk),
        grid=(bsz, dil, nb),
        in_specs=[cur, prev, cur, prev, cur],
        out_specs=[cur, cur],
        out_shape=[jax.ShapeDtypeStruct((bsz, l, dil * w), F32)] * 2,
        compiler_params=_params("parallel", "parallel", "arbitrary"),
        name=f"dilated_d{dil}",
    )(qv, kv, kv, vv, vv)
    return o.reshape(bsz * t, w), lse.reshape(bsz * t, w)


def _combine_out_kernel(og_ref, o1_ref, o2_ref, o3_ref, l1_ref, l2_ref, l3_ref, w_ref, r_ref, out_ref):
    l1, l2, l3 = l1_ref[...], l2_ref[...], l3_ref[...]
    mx = jnp.maximum(jnp.maximum(l1, l2), l3)
    e1, e2, e3 = jnp.exp(l1 - mx), jnp.exp(l2 - mx), jnp.exp(l3 - mx)
    att = (e1 * o1_ref[...] + e2 * o2_ref[...] + e3 * o3_ref[...]) / (e1 + e2 + e3)
    kg = og_ref.shape[1]
    acc = jnp.dot(og_ref[...].astype(BF16), w_ref[:kg, :], preferred_element_type=F32)
    acc = acc + jnp.dot(att.astype(BF16), w_ref[kg:, :], preferred_element_type=F32)
    out_ref[...] = r_ref[...] + acc


def combine_out(o_gla, outs, lses, w_bf16, res, tm=512):
    m = res.shape[0]
    n = w_bf16.shape[1]
    row = lambda w: pl.BlockSpec((tm, w), lambda i: (i, 0))
    return pl.pallas_call(
        _combine_out_kernel,
        grid=(m // tm,),
        in_specs=[row(GLA_V_W)] + [row(ATT_W)] * 6
                 + [pl.BlockSpec(w_bf16.shape, lambda i: (0, 0)), row(n)],
        out_specs=row(n),
        out_shape=jax.ShapeDtypeStruct((m, n), F32),
        compiler_params=_params("parallel"),
        name="combine_out",
    )(o_gla, *outs, *lses, w_bf16, res)


def _lru_kernel(gate_ref, xr_ref, cw_ref, cb_ref, wr_ref, br_ref, wi_ref, bi_ref, lam_ref, y_ref,
                tail_ref, h_ref, *, tt):
    @pl.when(pl.program_id(1) == 0)
    def _():
        tail_ref[...] = jnp.zeros_like(tail_ref)
        h_ref[...] = jnp.zeros_like(h_ref)

    xr = xr_ref[0]
    d = xr.shape[1]
    ext = jnp.concatenate([tail_ref[...], xr], axis=0)
    tail_ref[...] = xr[tt - 8:, :]
    xc = cb_ref[...] + jnp.zeros_like(xr)
    for j in range(CONV_W):
        off = 8 - (CONV_W - 1) + j
        xc = xc + cw_ref[j:j + 1, :] * ext[off:off + tt, :]
    xcb = xc.astype(BF16)
    rs, is_ = [], []
    for g in range(d // RNN_BLOCK_W):
        cs = slice(g * RNN_BLOCK_W, (g + 1) * RNN_BLOCK_W)
        rs.append(jnp.dot(xcb[:, cs], wr_ref[g], preferred_element_type=F32))
        is_.append(jnp.dot(xcb[:, cs], wi_ref[g], preferred_element_type=F32))
    r = jax.nn.sigmoid(jnp.concatenate(rs, axis=-1) + br_ref[...])
    ig = jax.nn.sigmoid(jnp.concatenate(is_, axis=-1) + bi_ref[...])
    lam = lam_ref[...]
    softplus = jnp.maximum(-lam, 0.0) + jnp.log1p(jnp.exp(-jnp.abs(lam)))
    log_a = (-LRU_C) * r * softplus
    a = jnp.exp(log_a)
    th = jnp.tanh(log_a)
    u = jnp.sqrt(-2.0 * th / (1.0 - th)) * (ig * xc)
    rows = lax.broadcasted_iota(jnp.int32, (tt, 1), 0)
    s = 1
    while s < tt:
        keep = rows >= s
        a_sh = jnp.where(keep, pltpu.roll(a, s, axis=0), 1.0)
        u_sh = jnp.where(keep, pltpu.roll(u, s, axis=0), 0.0)
        u = u + a * u_sh
        a = a * a_sh
        s *= 2
    hs = u + a * h_ref[...]
    h_ref[...] = hs[tt - 1:tt, :]
    y_ref[0] = _gelu_tanh(gate_ref[0]) * hs


def lru(gate, xr, conv_w, conv_b, w_r, b_r, w_i, b_i, lam, tt=256):
    bsz, t, d = xr.shape
    spec = pl.BlockSpec((1, tt, d), lambda b, i: (b, i, 0))
    row = pl.BlockSpec((1, d), lambda b, i: (0, 0))
    wspec = pl.BlockSpec(w_r.shape, lambda b, i: (0, 0, 0))
    return pl.pallas_call(
        functools.partial(_lru_kernel, tt=tt),
        grid=(bsz, t // tt),
        in_specs=[spec, spec, pl.BlockSpec((CONV_W, d), lambda b, i: (0, 0)), row,
                  wspec, row, wspec, row, row],
        out_specs=spec,
        out_shape=jax.ShapeDtypeStruct((bsz, t, d), F32),
        scratch_shapes=[pltpu.VMEM((8, d), F32), pltpu.VMEM((1, d), F32)],
        compiler_params=_params("parallel", "arbitrary"),
        name="lru",
    )(gate, xr, conv_w, conv_b.reshape(1, d), w_r.astype(BF16), b_r.reshape(1, d),
      w_i.astype(BF16), b_i.reshape(1, d), lam.reshape(1, d))


def _extract_top(vals, pos, n_rounds, payload=None):
    out_v, out_p, out_x = [], [], []
    for _ in range(n_rounds):
        m = jnp.max(vals, axis=0, keepdims=True)
        p = jnp.min(jnp.where(vals == m, pos, 1e9), axis=0, keepdims=True)
        first = pos == p
        if payload is not None:
            out_x.append(jnp.max(jnp.where(first, payload, -1.0), axis=0, keepdims=True))
        vals = jnp.where(first, -jnp.inf, vals)
        out_v.append(m)
        out_p.append(p)
    return out_v, out_p, out_x


def _peer_topk_kernel(q_ref, keys_ref, ids_ref, g_ref):
    k = PEER_TOPK
    tm = q_ref.shape[0]
    rows = lambda n: lax.broadcasted_iota(jnp.int32, (n, tm), 0).astype(F32)
    key_pos = rows(PEER_NKEYS)
    groups = [(0, 16)] + [(a, 8) for a in range(1, 8)]
    id_rows, g_rows = [], []
    for h in range(PEER_HEADS):
        tops = []
        for p in range(2):
            qs = q_ref[:, (2 * h + p) * LANES:(2 * h + p + 1) * LANES]
            s_t = lax.dot_general(keys_ref[h, p], qs, _NT, preferred_element_type=F32,
                                  precision=lax.Precision.HIGHEST)
            v, pidx, _ = _extract_top(s_t, key_pos, k)
            tops.append((jnp.concatenate(v, axis=0), jnp.concatenate(pidx, axis=0)))
        (t1, i1), (t2, i2) = tops
        cv, ci, cp = [], [], []
        for a, nb in groups:
            cv.append(t1[a:a + 1, :] + t2[0:nb, :])
            ci.append(i1[a:a + 1, :] * PEER_NKEYS + i2[0:nb, :])
            cp.append(a * k + rows(nb))
        cv.append(t1[8:16, :] + t2[0:1, :])
        ci.append(i1[8:16, :] * PEER_NKEYS + i2[0:1, :])
        cp.append((8.0 + rows(8)) * k)
        cand_v = jnp.concatenate(cv, axis=0)
        cand_i = jnp.concatenate(ci, axis=0)
        cand_p = jnp.concatenate(cp, axis=0)
        bv, _, bi = _extract_top(cand_v, cand_p, k, payload=cand_i)
        best = jnp.concatenate(bv, axis=0)
        e = jnp.exp(best - best[0:1, :])
        g_rows.append(e / jnp.sum(e, axis=0, keepdims=True))
        id_rows.append(jnp.concatenate(bi, axis=0))
    ids_t = jnp.concatenate(id_rows, axis=0)
    g_t = jnp.concatenate(g_rows, axis=0)
    for c in range(tm // LANES):
        cs = slice(c * LANES, (c + 1) * LANES)
        ids_ref[cs, :] = ids_t[:, cs].T.astype(jnp.int32)
        g_ref[cs, :] = g_t[:, cs].T


def peer_topk(q, sub_keys, tm=256):
    m = q.shape[0]
    return pl.pallas_call(
        _peer_topk_kernel,
        grid=(m // tm,),
        in_specs=[pl.BlockSpec((tm, q.shape[1]), lambda i: (i, 0)),
                  pl.BlockSpec(sub_keys.shape, lambda i: (0, 0, 0, 0))],
        out_specs=[pl.BlockSpec((tm, PEER_SLOTS), lambda i: (i, 0))] * 2,
        out_shape=[jax.ShapeDtypeStruct((m, PEER_SLOTS), jnp.int32),
                   jax.ShapeDtypeStruct((m, PEER_SLOTS), F32)],
        compiler_params=_params("parallel"),
        name="peer_topk",
    )(q, sub_keys)


SUBLANES = 8


def _transpose_sublane_blocks(x):
    x = list(x)
    sub = lax.broadcasted_iota(jnp.int32, x[0].shape, 0)
    s = SUBLANES // 2
    while s >= 1:
        clear = (sub & s) == 0
        for i in range(SUBLANES):
            if i & s:
                continue
            a, b = x[i], x[i + s]
            x[i] = jnp.where(clear, a, pltpu.roll(b, s, axis=0))
            x[i + s] = jnp.where(clear, pltpu.roll(a, SUBLANES - s, axis=0), b)
        s //= 2
    return x


def _peer_onehot_kernel(ids_ref, g_ref, w_ref):
    tm = ids_ref.shape[0]
    sub = lax.broadcasted_iota(jnp.int32, (PEER_NKEYS, PEER_SLOTS), 0)

    def body(grp, carry):
        m0 = pl.multiple_of(grp * SUBLANES, SUBLANES)
        res = []
        for t in range(SUBLANES):
            ids = ids_ref[pl.ds(m0 + t, 1), :]
            g = g_ref[pl.ds(m0 + t, 1), :]
            g_hi = g.astype(BF16).astype(F32)
            g_lo = g - g_hi
            hit1 = sub == (ids >> 7)
            hit2 = sub == (ids & (PEER_NKEYS - 1))
            c = jnp.concatenate([jnp.where(hit1, g_hi, 0.0), jnp.where(hit1, g_lo, 0.0)], axis=1)
            one = jnp.where(hit2, 1.0, 0.0)
            bt = jnp.concatenate([one, one], axis=1)
            res.append(lax.dot_general(c.astype(BF16), bt.astype(BF16), _NT, preferred_element_type=F32))
        for gi in range(PEER_NKEYS // SUBLANES):
            rows = slice(gi * SUBLANES, (gi + 1) * SUBLANES)
            y = _transpose_sublane_blocks([r[rows, :] for r in res])
            for r in range(SUBLANES):
                w_ref[gi * SUBLANES + r, pl.ds(m0, SUBLANES), :] = y[r]
        return carry

    lax.fori_loop(0, tm // SUBLANES, body, 0)


def peer_onehot(ids, g, tm=128):
    m = ids.shape[0]
    return pl.pallas_call(
        _peer_onehot_kernel,
        grid=(m // tm,),
        in_specs=[pl.BlockSpec((tm, PEER_SLOTS), lambda i: (i, 0))] * 2,
        out_specs=pl.BlockSpec((PEER_NKEYS, tm, PEER_NKEYS), lambda i: (0, i, 0)),
        out_shape=jax.ShapeDtypeStruct((PEER_NKEYS, m, PEER_NKEYS), F32),
        compiler_params=_params("parallel"),
        name="peer_onehot",
    )(ids, g)


def _peer_dense_kernel(h_ref, gain_ref, w_ref, u_ref, v_ref, fg_ref, o_ref, hn_ref, acc_ref, *,
                       groups, final_norm):
    j = pl.program_id(1)

    @pl.when(j == 0)
    def _():
        hn_ref[...] = _rms(h_ref[...], gain_ref[...]).astype(BF16)
        acc_ref[...] = jnp.zeros_like(acc_ref)

    z = jnp.dot(hn_ref[...], u_ref[...], preferred_element_type=F32)
    wsel = jnp.concatenate([w_ref[gi] for gi in range(groups)], axis=-1)
    act = (_gelu_tanh(z) * wsel).astype(BF16)
    acc_ref[...] += jnp.dot(act, v_ref[...], preferred_element_type=F32)

    @pl.when(j == pl.num_programs(1) - 1)
    def _():
        out = h_ref[...] + acc_ref[...]
        if final_norm:
            out = _rms(out, fg_ref[...])
        o_ref[...] = out


def peer_dense(h, gain, wmap, ut_bf16, v_bf16, final_gain, final_norm, tm=512, groups=8):
    m, d = h.shape
    ne = v_bf16.shape[0]
    te = groups * PEER_NKEYS
    return pl.pallas_call(
        functools.partial(_peer_dense_kernel, groups=groups, final_norm=final_norm),
        grid=(m // tm, ne // te),
        in_specs=[pl.BlockSpec((tm, d), lambda i, j: (i, 0)),
                  pl.BlockSpec((1, d), lambda i, j: (0, 0)),
                  pl.BlockSpec((groups, tm, PEER_NKEYS), lambda i, j: (j, i, 0)),
                  pl.BlockSpec((d, te), lambda i, j: (0, j)),
                  pl.BlockSpec((te, d), lambda i, j: (j, 0)),
                  pl.BlockSpec((1, d), lambda i, j: (0, 0))],
        out_specs=pl.BlockSpec((tm, d), lambda i, j: (i, 0)),
        out_shape=jax.ShapeDtypeStruct((m, d), F32),
        scratch_shapes=[pltpu.VMEM((tm, d), BF16), pltpu.VMEM((tm, d), F32)],
        compiler_params=_params("parallel", "arbitrary"),
        name="peer_dense",
    )(h, gain.reshape(1, d), wmap, ut_bf16, v_bf16, final_gain.reshape(1, d))


def peer_ffn(h, gain, w_query, sub_keys, u_tab, v_tab, final_gain, final_norm):
    (q,) = norm_matmul(h, gain, w_query.astype(BF16), [w_query.shape[1]], [F32])
    ids, g = peer_topk(q, sub_keys)
    wmap = peer_onehot(ids, g)
    return peer_dense(h, gain, wmap, u_tab.astype(BF16).T, v_tab.astype(BF16), final_gain, final_norm)


def gla_dilated_layer(h, bsz, gain, w_in, w_alpha, b_alpha, head_gain, w_out):
    m, d = h.shape
    t = m // bsz
    sizes = [GLA_QK_W, GLA_QK_W, GLA_V_W, GLA_V_W, GLA_GATE_RANK, ATT_W, ATT_W, ATT_W]
    offs = [0]
    for s in sizes:
        offs.append(offs[-1] + s)
    seg = lambda i: w_in[:, offs[i]:offs[i + 1]]
    pad = jnp.zeros((d, LANES - GLA_GATE_RANK), w_in.dtype)
    w_cat = jnp.concatenate([seg(0), seg(1), seg(2), seg(3), seg(5), seg(6), seg(7), seg(4), pad], axis=1)
    widths = [GLA_QK_W, GLA_QK_W, GLA_V_W, GLA_V_W, ATT_W, ATT_W, ATT_W, LANES]
    dts = [F32, F32, F32, F32, BF16, BF16, BF16, F32]
    gq, gk, gv, gg, aq, ak, av, glr = norm_matmul(h, gain, w_cat.astype(BF16), widths, dts)
    r3 = lambda a: a.reshape(bsz, t, a.shape[1])
    o_gla = gla(r3(gq), r3(gk), r3(gv), r3(gg), r3(glr), w_alpha, b_alpha, head_gain)
    outs, lses = [], []
    for window, dil in DILATED_BRANCHES:
        o, l = dilated_branch(r3(aq), r3(ak), r3(av), window, dil)
        outs.append(o)
        lses.append(l)
    return combine_out(o_gla.reshape(m, GLA_V_W), outs, lses, w_out.astype(BF16), h)


def rglru_layer(h, bsz, gain, w_in, conv_w, conv_b, w_r, b_r, w_i, b_i, lam, w_out):
    m, d = h.shape
    t = m // bsz
    d_rnn = w_in.shape[1] // 2
    gate, xr = norm_matmul(h, gain, w_in.astype(BF16), [d_rnn, d_rnn], [F32, F32])
    y = lru(gate.reshape(bsz, t, d_rnn), xr.reshape(bsz, t, d_rnn), conv_w, conv_b, w_r, b_r, w_i, b_i, lam)
    return matmul_res(y.reshape(m, d_rnn), w_out.astype(BF16), h)


def kernel(x, mix_norm, ffn_norm, ab_w_in, gla_w_alpha, gla_b_alpha, gla_head_gain, ab_w_out,
           c_w_in, c_conv_w, c_conv_b, c_w_rgate, c_b_rgate, c_w_igate, c_b_igate, c_lambda, c_w_out,
           peer_w_query, peer_sub_keys, peer_u, peer_v, final_norm):
    bsz, t, d = x.shape
    depth = mix_norm.shape[0]
    h = x.reshape(bsz * t, d)
    for layer in range(depth):
        if layer % 2 == 0:
            e = layer // 2
            h = gla_dilated_layer(h, bsz, mix_norm[layer], ab_w_in[e], gla_w_alpha[e], gla_b_alpha[e],
                                  gla_head_gain[e], ab_w_out[e])
        else:
            o = layer // 2
            h = rglru_layer(h, bsz, mix_norm[layer], c_w_in[o], c_conv_w[o], c_conv_b[o], c_w_rgate[o],
                            c_b_rgate[o], c_w_igate[o], c_b_igate[o], c_lambda[o], c_w_out[o])
        h = peer_ffn(h, ffn_norm[layer], peer_w_query[layer], peer_sub_keys[layer], peer_u[layer],
                     peer_v[layer], final_norm, final_norm=(layer == depth - 1))
    return h.reshape(bsz, t, d)
```

```python
import functools
import math

import jax
import jax.numpy as jnp
from jax import lax
from jax.experimental import pallas as pl
from jax.experimental.pallas import tpu as pltpu

F32 = jnp.float32
BF16 = jnp.bfloat16

NORM_EPS = 1e-6
V7X_VMEM_LIMIT = 56 * 1024 * 1024

GLA_HEADS = 4
GLA_DK = 64
GLA_DV = 128
GLA_GATE_RANK = 16
GLA_GATE_TAU = 16.0
GLA_CHUNK = 64
ATT_HEADS = 8
ATT_HD = 64
DILATED_BRANCHES = ((128, 1), (512, 4), (2048, 16))
GLA_QK_W = GLA_HEADS * GLA_DK
GLA_V_W = GLA_HEADS * GLA_DV
ATT_W = ATT_HEADS * ATT_HD
CONV_W = 4
LRU_C = 8.0
RNN_BLOCK_W = 128
PEER_HEADS = 8
PEER_NKEYS = 128
PEER_TOPK = 16
PEER_SLOTS = PEER_HEADS * PEER_TOPK
LANES = 128
SUBLANES = 8

_NT = (((1,), (1,)), ((), ()))
_TN = (((0,), (0,)), ((), ()))


def _params(*sem):
    return pltpu.CompilerParams(dimension_semantics=sem, vmem_limit_bytes=V7X_VMEM_LIMIT)


def _gelu_tanh(x):
    return 0.5 * x * (1.0 + jnp.tanh(math.sqrt(2.0 / math.pi) * (x + 0.044715 * (x * x * x))))


def _rms(x, gain):
    ms = jnp.mean(x * x, axis=-1, keepdims=True)
    return x * lax.rsqrt(ms + NORM_EPS) * gain


def _norm_matmul_kernel(x_ref, g_ref, w_ref, *o_refs, widths):
    xn = _rms(x_ref[...], g_ref[...]).astype(BF16)
    off = 0
    for o_ref, w in zip(o_refs, widths):
        y = jnp.dot(xn, w_ref[:, off:off + w], preferred_element_type=F32).astype(o_ref.dtype)
        if len(o_ref.shape) == 3:
            for c in range(w // LANES):
                o_ref[c] = y[:, c * LANES:(c + 1) * LANES]
        else:
            o_ref[...] = y
        off += w


def norm_matmul(x, gain, w_bf16, widths, dtypes, tm=512, lane_major=()):
    m, d = x.shape
    n = w_bf16.shape[1]
    assert sum(widths) == n and m % tm == 0
    out_specs, out_shape = [], []
    for k, (w, dt) in enumerate(zip(widths, dtypes)):
        if k in lane_major:
            out_specs.append(pl.BlockSpec((w // LANES, tm, LANES), lambda i: (0, i, 0)))
            out_shape.append(jax.ShapeDtypeStruct((w // LANES, m, LANES), dt))
        else:
            out_specs.append(pl.BlockSpec((tm, w), lambda i: (i, 0)))
            out_shape.append(jax.ShapeDtypeStruct((m, w), dt))
    return pl.pallas_call(
        functools.partial(_norm_matmul_kernel, widths=tuple(widths)),
        grid=(m // tm,),
        in_specs=[pl.BlockSpec((tm, d), lambda i: (i, 0)),
                  pl.BlockSpec((1, d), lambda i: (0, 0)),
                  pl.BlockSpec((d, n), lambda i: (0, 0))],
        out_specs=out_specs,
        out_shape=out_shape,
        compiler_params=_params("parallel"),
        name="norm_matmul",
    )(x, gain.reshape(1, d), w_bf16)


def _matmul_res_kernel(x_ref, w_ref, r_ref, o_ref):
    o_ref[...] = r_ref[...] + jnp.dot(x_ref[...].astype(BF16), w_ref[...], preferred_element_type=F32)


def matmul_res(x, w_bf16, res, tm=512):
    m, k = x.shape
    n = w_bf16.shape[1]
    return pl.pallas_call(
        _matmul_res_kernel,
        grid=(m // tm,),
        in_specs=[pl.BlockSpec((tm, k), lambda i: (i, 0)),
                  pl.BlockSpec((k, n), lambda i: (0, 0)),
                  pl.BlockSpec((tm, n), lambda i: (i, 0))],
        out_specs=pl.BlockSpec((tm, n), lambda i: (i, 0)),
        out_shape=jax.ShapeDtypeStruct((m, n), F32),
        compiler_params=_params("parallel"),
        name="matmul_res",
    )(x, w_bf16, res)


def _gla_kernel(q_ref, k_ref, v_ref, gg_ref, lr_ref, wa_ref, ba_ref, hg_ref, o_ref, st_ref, *, tg):
    c = GLA_CHUNK

    @pl.when(pl.program_id(1) == 0)
    def _():
        st_ref[...] = jnp.zeros_like(st_ref)

    z = jnp.dot(lr_ref[0], wa_ref[...], preferred_element_type=F32,
                precision=lax.Precision.HIGHEST) + ba_ref[...]
    log_a = (jnp.minimum(z, 0.0) - jnp.log1p(jnp.exp(-jnp.abs(z)))) * (1.0 / GLA_GATE_TAU)
    row = lax.broadcasted_iota(jnp.int32, (c, c), 0)
    col = lax.broadcasted_iota(jnp.int32, (c, c), 1)
    causal = row >= col
    tril = causal.astype(F32)
    for ci in range(tg // c):
        sl = slice(ci * c, (ci + 1) * c)
        b = jnp.dot(tril, log_a[sl], preferred_element_type=F32, precision=lax.Precision.HIGHEST)
        b_last = b[c - 1:c, :]
        eb = jnp.exp(b)
        q_dec = q_ref[0, sl, :] * (GLA_DK ** -0.5) * eb
        kk = k_ref[0, sl, :]
        k_inv = kk * jnp.exp(-b)
        k_end = kk * jnp.exp(b_last - b)
        decay = jnp.exp(b_last)
        vv = v_ref[0, sl, :]
        outs = []
        for h in range(GLA_HEADS):
            ks = slice(h * GLA_DK, (h + 1) * GLA_DK)
            vs = slice(h * GLA_DV, (h + 1) * GLA_DV)
            qh = q_dec[:, ks].astype(BF16)
            vh = vv[:, vs].astype(BF16)
            att = lax.dot_general(qh, k_inv[:, ks].astype(BF16), _NT, preferred_element_type=F32)
            att = jnp.where(causal, att, 0.0)
            st = st_ref[h]
            o_h = jnp.dot(att.astype(BF16), vh, preferred_element_type=F32)
            o_h = o_h + lax.dot_general(qh, st.astype(BF16), _NT, preferred_element_type=F32)
            kv_t = lax.dot_general(vh, k_end[:, ks].astype(BF16), _TN, preferred_element_type=F32)
            st_ref[h] = decay[:, ks] * st + kv_t
            o_h = o_h * lax.rsqrt(jnp.mean(o_h * o_h, axis=-1, keepdims=True) + NORM_EPS)
            outs.append(o_h)
        o = jnp.concatenate(outs, axis=-1)
        gg = gg_ref[0, sl, :]
        o_ref[0, sl, :] = o * hg_ref[...] * (gg * jax.nn.sigmoid(gg))


def gla(gq, gk, gv, gg, glr, w_alpha, b_alpha, head_gain, tg=256):
    bsz, t, _ = gq.shape
    wa = jnp.zeros((LANES, GLA_QK_W), F32).at[:GLA_GATE_RANK].set(w_alpha)
    spec = lambda w: pl.BlockSpec((1, tg, w), lambda b, i: (b, i, 0))
    const = lambda r, w: pl.BlockSpec((r, w), lambda b, i: (0, 0))
    return pl.pallas_call(
        functools.partial(_gla_kernel, tg=tg),
        grid=(bsz, t // tg),
        in_specs=[spec(GLA_QK_W), spec(GLA_QK_W), spec(GLA_V_W), spec(GLA_V_W), spec(LANES),
                  const(LANES, GLA_QK_W), const(1, GLA_QK_W), const(1, GLA_V_W)],
        out_specs=spec(GLA_V_W),
        out_shape=jax.ShapeDtypeStruct((bsz, t, GLA_V_W), F32),
        scratch_shapes=[pltpu.VMEM((GLA_HEADS, GLA_DV, GLA_DK), F32)],
        compiler_params=_params("parallel", "arbitrary"),
        name="gla",
    )(gq, gk, gv, gg, glr, wa, b_alpha.reshape(1, -1), head_gain.reshape(1, -1))


def _dilated_kernel(q_ref, kp_ref, kc_ref, vp_ref, vc_ref, o_ref, l_ref, *, dil, blk):
    i = pl.program_id(1)
    cblk = pl.program_id(2)
    n_pairs = q_ref.shape[0]
    n_cblk = ATT_HEADS // 2 // n_pairs
    qi = lax.broadcasted_iota(jnp.int32, (blk, 2 * blk), 0)
    kj = lax.broadcasted_iota(jnp.int32, (blk, 2 * blk), 1)
    delta = qi + blk - kj
    kj_min = jnp.where(i > 0, 0, blk)
    valid = (delta >= 0) & (delta <= blk) & (kj >= kj_min)
    dist = (delta * dil).astype(F32)
    lane = lax.broadcasted_iota(jnp.int32, (1, 2 * ATT_HD), 1)
    lo = lane < ATT_HD

    def alibi(lp, e):
        slopes = [2.0 ** (-8.0 * (2 * (c * n_pairs + lp) + e + 1) / ATT_HEADS) for c in range(n_cblk)]
        slope = jnp.float32(slopes[-1])
        for c in range(n_cblk - 2, -1, -1):
            slope = jnp.where(cblk == c, jnp.float32(slopes[c]), slope)
        return jnp.where(valid, -slope * dist, -1e30)

    biases = [[alibi(lp, e) for e in range(2)] for lp in range(n_pairs)]

    def body(r, carry):
        rows = pl.ds(r, blk, stride=dil)
        for lp in range(n_pairs):
            q2 = q_ref[lp, rows, :].astype(BF16)
            k2 = jnp.concatenate([kp_ref[lp, rows, :], kc_ref[lp, rows, :]], axis=0).astype(BF16)
            v2 = jnp.concatenate([vp_ref[lp, rows, :], vc_ref[lp, rows, :]], axis=0).astype(BF16)
            o_pair = jnp.zeros((blk, 2 * ATT_HD), F32)
            lse_pair = jnp.zeros((blk, 2 * ATT_HD), F32)
            for e in range(2):
                sel = lo if e == 0 else jnp.logical_not(lo)
                qm = jnp.where(sel, q2, jnp.zeros_like(q2))
                vm = jnp.where(sel, v2, jnp.zeros_like(v2))
                s = lax.dot_general(qm, k2, _NT, preferred_element_type=F32) * (ATT_HD ** -0.5)
                s = s + biases[lp][e]
                m = jnp.max(s, axis=-1, keepdims=True)
                pexp = jnp.exp(s - m)
                l = jnp.sum(pexp, axis=-1, keepdims=True)
                o_e = jnp.dot(pexp.astype(BF16), vm, preferred_element_type=F32)
                o_pair = o_pair + o_e * (1.0 / l)
                lse_pair = jnp.where(sel, m + jnp.log(l), lse_pair)
            o_ref[lp, rows, :] = o_pair
            l_ref[lp, rows, :] = lse_pair
        return carry

    if dil == 1:
        body(0, 0)
    else:
        lax.fori_loop(0, dil, body, 0)


def dilated_branch(q, k, v, bsz, window, dil):
    n_slab, m, pw = q.shape
    t = m // bsz
    blk = window // dil
    rows = blk * dil
    assert t % rows == 0 and pw == 2 * ATT_HD
    nblk = t // rows
    n_pairs = n_slab if rows <= 512 else n_slab // 2
    cur = pl.BlockSpec((n_pairs, rows, pw), lambda b, i, c: (c, b * nblk + i, 0))
    prev = pl.BlockSpec((n_pairs, rows, pw), lambda b, i, c: (c, b * nblk + jnp.maximum(i - 1, 0), 0))
    return pl.pallas_call(
        functools.partial(_dilated_kernel, dil=dil, blk=blk),
        grid=(bsz, nblk, n_slab // n_pairs),
        in_specs=[cur, prev, cur, prev, cur],
        out_specs=[cur, cur],
        out_shape=[jax.ShapeDtypeStruct((n_slab, m, pw), F32)] * 2,
        compiler_params=_params("parallel", "arbitrary", "arbitrary"),
        name=f"dilated_d{dil}",
    )(q, k, k, v, v)


def _combine_out_kernel(og_ref, o1_ref, o2_ref, o3_ref, l1_ref, l2_ref, l3_ref, w_ref, r_ref, out_ref):
    wide = lambda ref: jnp.concatenate([ref[c] for c in range(ref.shape[0])], axis=-1)
    l1, l2, l3 = wide(l1_ref), wide(l2_ref), wide(l3_ref)
    mx = jnp.maximum(jnp.maximum(l1, l2), l3)
    e1, e2, e3 = jnp.exp(l1 - mx), jnp.exp(l2 - mx), jnp.exp(l3 - mx)
    att = (e1 * wide(o1_ref) + e2 * wide(o2_ref) + e3 * wide(o3_ref)) / (e1 + e2 + e3)
    kg = og_ref.shape[1]
    acc = jnp.dot(og_ref[...].astype(BF16), w_ref[:kg, :], preferred_element_type=F32)
    acc = acc + jnp.dot(att.astype(BF16), w_ref[kg:, :], preferred_element_type=F32)
    out_ref[...] = r_ref[...] + acc


def combine_out(o_gla, outs, lses, w_bf16, res, tm=512):
    m = res.shape[0]
    n = w_bf16.shape[1]
    row = lambda w: pl.BlockSpec((tm, w), lambda i: (i, 0))
    slab = pl.BlockSpec((ATT_W // LANES, tm, LANES), lambda i: (0, i, 0))
    return pl.pallas_call(
        _combine_out_kernel,
        grid=(m // tm,),
        in_specs=[row(GLA_V_W)] + [slab] * 6
                 + [pl.BlockSpec(w_bf16.shape, lambda i: (0, 0)), row(n)],
        out_specs=row(n),
        out_shape=jax.ShapeDtypeStruct((m, n), F32),
        compiler_params=_params("parallel"),
        name="combine_out",
    )(o_gla, *outs, *lses, w_bf16, res)


def _lru_kernel(gate_ref, xr_ref, cw_ref, cb_ref, wr_ref, br_ref, wi_ref, bi_ref, lam_ref, y_ref,
                ext_ref, h_ref, *, tt):
    @pl.when(pl.program_id(1) == 0)
    def _():
        ext_ref[:SUBLANES, :] = jnp.zeros((SUBLANES, ext_ref.shape[1]), F32)
        h_ref[...] = jnp.zeros_like(h_ref)

    xr = xr_ref[0]
    d = xr.shape[1]
    ext_ref[SUBLANES:, :] = xr
    xc = cb_ref[...] + cw_ref[CONV_W - 1:CONV_W, :] * xr
    for j in range(CONV_W - 1):
        off = SUBLANES - (CONV_W - 1) + j
        xc = xc + cw_ref[j:j + 1, :] * ext_ref[off:off + tt, :]
    ext_ref[:SUBLANES, :] = xr[tt - SUBLANES:, :]
    xcb = xc.astype(BF16)
    rs, is_ = [], []
    for g in range(d // RNN_BLOCK_W):
        cs = slice(g * RNN_BLOCK_W, (g + 1) * RNN_BLOCK_W)
        rs.append(jnp.dot(xcb[:, cs], wr_ref[g], preferred_element_type=F32))
        is_.append(jnp.dot(xcb[:, cs], wi_ref[g], preferred_element_type=F32))
    r = jax.nn.sigmoid(jnp.concatenate(rs, axis=-1) + br_ref[...])
    ig = jax.nn.sigmoid(jnp.concatenate(is_, axis=-1) + bi_ref[...])
    lam = lam_ref[...]
    softplus = jnp.maximum(-lam, 0.0) + jnp.log1p(jnp.exp(-jnp.abs(lam)))
    log_a = (-LRU_C) * r * softplus
    a = jnp.exp(log_a)
    th = jnp.tanh(log_a)
    n = -2.0 * th
    u = jnp.where(n > 0.0, n * lax.rsqrt(n), 0.0) * lax.rsqrt(1.0 - th) * (ig * xc)
    a = a.reshape(tt // SUBLANES, SUBLANES, d)
    u = u.reshape(tt // SUBLANES, SUBLANES, d)
    rows = lax.broadcasted_iota(jnp.int32, (1, SUBLANES, 1), 1)
    s = 1
    while s < SUBLANES:
        keep = rows >= s
        a_sh = jnp.where(keep, pltpu.roll(a, s, axis=1), 1.0)
        u_sh = jnp.where(keep, pltpu.roll(u, s, axis=1), 0.0)
        u = u + a * u_sh
        a = a * a_sh
        s *= 2
    carry = h_ref[...]
    gate = gate_ref[0]
    for g in range(tt // SUBLANES):
        hs = u[g] + a[g] * carry
        carry = hs[SUBLANES - 1:SUBLANES, :]
        y_ref[0, g * SUBLANES:(g + 1) * SUBLANES, :] = _gelu_tanh(gate[g * SUBLANES:(g + 1) * SUBLANES]) * hs
    h_ref[...] = carry


def lru(gate, xr, conv_w, conv_b, w_r, b_r, w_i, b_i, lam, tt=256):
    bsz, t, d = xr.shape
    spec = pl.BlockSpec((1, tt, d), lambda b, i: (b, i, 0))
    row = pl.BlockSpec((1, d), lambda b, i: (0, 0))
    wspec = pl.BlockSpec(w_r.shape, lambda b, i: (0, 0, 0))
    return pl.pallas_call(
        functools.partial(_lru_kernel, tt=tt),
        grid=(bsz, t // tt),
        in_specs=[spec, spec, pl.BlockSpec((CONV_W, d), lambda b, i: (0, 0)), row,
                  wspec, row, wspec, row, row],
        out_specs=spec,
        out_shape=jax.ShapeDtypeStruct((bsz, t, d), F32),
        scratch_shapes=[pltpu.VMEM((tt + SUBLANES, d), F32), pltpu.VMEM((1, d), F32)],
        compiler_params=_params("parallel", "arbitrary"),
        name="lru",
    )(gate, xr, conv_w, conv_b.reshape(1, d), w_r.astype(BF16), b_r.reshape(1, d),
      w_i.astype(BF16), b_i.reshape(1, d), lam.reshape(1, d))


def _extract_top(vals, pos, n_rounds, payload=None):
    out_v, out_p, out_x = [], [], []
    for _ in range(n_rounds):
        m = jnp.max(vals, axis=0, keepdims=True)
        p = jnp.min(jnp.where(vals == m, pos, 1e9), axis=0, keepdims=True)
        first = pos == p
        if payload is not None:
            out_x.append(jnp.max(jnp.where(first, payload, -1.0), axis=0, keepdims=True))
        vals = jnp.where(first, -jnp.inf, vals)
        out_v.append(m)
        out_p.append(p)
    return out_v, out_p, out_x


def _peer_topk_kernel(q_ref, keys_ref, ids_ref, g_ref):
    k = PEER_TOPK
    tm = q_ref.shape[0]
    rows = lambda n: lax.broadcasted_iota(jnp.int32, (n, tm), 0).astype(F32)
    key_pos = rows(PEER_NKEYS)
    groups = [(0, 16)] + [(a, 8) for a in range(1, 8)]
    id_rows, g_rows = [], []
    for h in range(PEER_HEADS):
        tops = []
        for p in range(2):
            qs = q_ref[:, (2 * h + p) * LANES:(2 * h + p + 1) * LANES]
            s_t = lax.dot_general(keys_ref[h, p], qs, _NT, preferred_element_type=F32,
                                  precision=lax.Precision.HIGHEST)
            v, pidx, _ = _extract_top(s_t, key_pos, k)
            tops.append((jnp.concatenate(v, axis=0), jnp.concatenate(pidx, axis=0)))
        (t1, i1), (t2, i2) = tops
        cv, ci, cp = [], [], []
        for a, nb in groups:
            cv.append(t1[a:a + 1, :] + t2[0:nb, :])
            ci.append(i1[a:a + 1, :] * PEER_NKEYS + i2[0:nb, :])
            cp.append(a * k + rows(nb))
        cv.append(t1[8:16, :] + t2[0:1, :])
        ci.append(i1[8:16, :] * PEER_NKEYS + i2[0:1, :])
        cp.append((8.0 + rows(8)) * k)
        cand_v = jnp.concatenate(cv, axis=0)
        cand_i = jnp.concatenate(ci, axis=0)
        cand_p = jnp.concatenate(cp, axis=0)
        bv, _, bi = _extract_top(cand_v, cand_p, k, payload=cand_i)
        best = jnp.concatenate(bv, axis=0)
        e = jnp.exp(best - best[0:1, :])
        g_rows.append(e / jnp.sum(e, axis=0, keepdims=True))
        id_rows.append(jnp.concatenate(bi, axis=0))
    ids_t = jnp.concatenate(id_rows, axis=0)
    g_t = jnp.concatenate(g_rows, axis=0)
    for c in range(tm // LANES):
        cs = slice(c * LANES, (c + 1) * LANES)
        ids_ref[cs, :] = ids_t[:, cs].T.astype(jnp.int32)
        g_ref[cs, :] = g_t[:, cs].T


def peer_topk(q, sub_keys, tm=256):
    m = q.shape[0]
    return pl.pallas_call(
        _peer_topk_kernel,
        grid=(m // tm,),
        in_specs=[pl.BlockSpec((tm, q.shape[1]), lambda i: (i, 0)),
                  pl.BlockSpec(sub_keys.shape, lambda i: (0, 0, 0, 0))],
        out_specs=[pl.BlockSpec((tm, PEER_SLOTS), lambda i: (i, 0))] * 2,
        out_shape=[jax.ShapeDtypeStruct((m, PEER_SLOTS), jnp.int32),
                   jax.ShapeDtypeStruct((m, PEER_SLOTS), F32)],
        compiler_params=_params("parallel"),
        name="peer_topk",
    )(q, sub_keys)


def _transpose_sublane_blocks(x):
    x = list(x)
    sub = lax.broadcasted_iota(jnp.int32, x[0].shape, 0)
    s = SUBLANES // 2
    while s >= 1:
        clear = (sub & s) == 0
        for i in range(SUBLANES):
            if i & s:
                continue
            a, b = x[i], x[i + s]
            x[i] = jnp.where(clear, a, pltpu.roll(b, s, axis=0))
            x[i + s] = jnp.where(clear, pltpu.roll(a, SUBLANES - s, axis=0), b)
        s //= 2
    return x


def _peer_onehot_kernel(ids_ref, g_ref, w_ref, ra_ref, rb_ref):
    tm = ids_ref.shape[0]
    n_grp = tm // SUBLANES
    sub = lax.broadcasted_iota(jnp.int32, (PEER_NKEYS, PEER_SLOTS), 0)

    def maps(grp, r_ref):
        m0 = pl.multiple_of(grp * SUBLANES, SUBLANES)
        for t in range(SUBLANES):
            ids = ids_ref[pl.ds(m0 + t, 1), :]
            g = g_ref[pl.ds(m0 + t, 1), :]
            g_hi = g.astype(BF16).astype(F32)
            g_lo = g - g_hi
            hit1 = sub == (ids >> 7)
            hit2 = sub == (ids & (PEER_NKEYS - 1))
            c = jnp.concatenate([jnp.where(hit1, g_hi, 0.0), jnp.where(hit1, g_lo, 0.0)], axis=1)
            one = jnp.where(hit2, 1.0, 0.0)
            bt = jnp.concatenate([one, one], axis=1)
            r_ref[t] = lax.dot_general(c.astype(BF16), bt.astype(BF16), _NT, preferred_element_type=F32)

    def emit(grp, r_ref):
        m0 = pl.multiple_of(grp * SUBLANES, SUBLANES)
        for gi in range(PEER_NKEYS // SUBLANES):
            rows = slice(gi * SUBLANES, (gi + 1) * SUBLANES)
            y = _transpose_sublane_blocks([r_ref[t, rows, :] for t in range(SUBLANES)])
            for r in range(SUBLANES):
                w_ref[gi * SUBLANES + r, pl.ds(m0, SUBLANES), :] = y[r]

    maps(0, ra_ref)

    def body(k, carry):
        g0 = 2 * k
        maps(g0 + 1, rb_ref)
        emit(g0, ra_ref)
        maps(g0 + 2, ra_ref)
        emit(g0 + 1, rb_ref)
        return carry

    lax.fori_loop(0, n_grp // 2 - 1, body, 0)
    maps(n_grp - 1, rb_ref)
    emit(n_grp - 2, ra_ref)
    emit(n_grp - 1, rb_ref)


def peer_onehot(ids, g, tm=128):
    m = ids.shape[0]
    assert tm % (2 * SUBLANES) == 0
    res = pltpu.VMEM((SUBLANES, PEER_NKEYS, PEER_NKEYS), F32)
    return pl.pallas_call(
        _peer_onehot_kernel,
        grid=(m // tm,),
        in_specs=[pl.BlockSpec((tm, PEER_SLOTS), lambda i: (i, 0))] * 2,
        out_specs=pl.BlockSpec((PEER_NKEYS, tm, PEER_NKEYS), lambda i: (0, i, 0)),
        out_shape=jax.ShapeDtypeStruct((PEER_NKEYS, m, PEER_NKEYS), F32),
        scratch_shapes=[res, res],
        compiler_params=_params("parallel"),
        name="peer_onehot",
    )(ids, g)


def _peer_dense_kernel(h_ref, gain_ref, w_ref, u_ref, v_ref, fg_ref, o_ref, hn_ref, acc_ref, *,
                       groups, final_norm):
    j = pl.program_id(1)

    @pl.when(j == 0)
    def _():
        hn_ref[...] = _rms(h_ref[...], gain_ref[...]).astype(BF16)
        acc_ref[...] = jnp.zeros_like(acc_ref)

    z = jnp.dot(hn_ref[...], u_ref[...], preferred_element_type=F32)
    wsel = jnp.concatenate([w_ref[gi] for gi in range(groups)], axis=-1)
    act = (_gelu_tanh(z) * wsel).astype(BF16)
    acc_ref[...] += jnp.dot(act, v_ref[...], preferred_element_type=F32)

    @pl.when(j == pl.num_programs(1) - 1)
    def _():
        out = h_ref[...] + acc_ref[...]
        if final_norm:
            out = _rms(out, fg_ref[...])
        o_ref[...] = out


def peer_dense(h, gain, wmap, ut_bf16, v_bf16, final_gain, final_norm, tm=512, groups=8):
    m, d = h.shape
    ne = v_bf16.shape[0]
    te = groups * PEER_NKEYS
    return pl.pallas_call(
        functools.partial(_peer_dense_kernel, groups=groups, final_norm=final_norm),
        grid=(m // tm, ne // te),
        in_specs=[pl.BlockSpec((tm, d), lambda i, j: (i, 0)),
                  pl.BlockSpec((1, d), lambda i, j: (0, 0)),
                  pl.BlockSpec((groups, tm, PEER_NKEYS), lambda i, j: (j, i, 0)),
                  pl.BlockSpec((d, te), lambda i, j: (0, j)),
                  pl.BlockSpec((te, d), lambda i, j: (j, 0)),
                  pl.BlockSpec((1, d), lambda i, j: (0, 0))],
        out_specs=pl.BlockSpec((tm, d), lambda i, j: (i, 0)),
        out_shape=jax.ShapeDtypeStruct((m, d), F32),
        scratch_shapes=[pltpu.VMEM((tm, d), BF16), pltpu.VMEM((tm, d), F32)],
        compiler_params=_params("parallel", "arbitrary"),
        name="peer_dense",
    )(h, gain.reshape(1, d), wmap, ut_bf16, v_bf16, final_gain.reshape(1, d))


def peer_ffn(h, gain, w_query, sub_keys, u_tab, v_tab, final_gain, final_norm):
    (q,) = norm_matmul(h, gain, w_query.astype(BF16), [w_query.shape[1]], [F32])
    ids, g = peer_topk(q, sub_keys)
    wmap = peer_onehot(ids, g)
    return peer_dense(h, gain, wmap, u_tab.astype(BF16).T, v_tab.astype(BF16), final_gain, final_norm)


def gla_dilated_layer(h, bsz, gain, w_in, w_alpha, b_alpha, head_gain, w_out):
    m, d = h.shape
    t = m // bsz
    sizes = [GLA_QK_W, GLA_QK_W, GLA_V_W, GLA_V_W, GLA_GATE_RANK, ATT_W, ATT_W, ATT_W]
    offs = [0]
    for s in sizes:
        offs.append(offs[-1] + s)
    seg = lambda i: w_in[:, offs[i]:offs[i + 1]]
    pad = jnp.zeros((d, LANES - GLA_GATE_RANK), w_in.dtype)
    w_cat = jnp.concatenate([seg(0), seg(1), seg(2), seg(3), seg(5), seg(6), seg(7), seg(4), pad], axis=1)
    widths = [GLA_QK_W, GLA_QK_W, GLA_V_W, GLA_V_W, ATT_W, ATT_W, ATT_W, LANES]
    dts = [F32] * len(widths)
    gq, gk, gv, gg, aq, ak, av, glr = norm_matmul(h, gain, w_cat.astype(BF16), widths, dts,
                                                  lane_major=(4, 5, 6))
    r3 = lambda a: a.reshape(bsz, t, a.shape[1])
    o_gla = gla(r3(gq), r3(gk), r3(gv), r3(gg), r3(glr), w_alpha, b_alpha, head_gain)
    outs, lses = [], []
    for window, dil in DILATED_BRANCHES:
        o, l = dilated_branch(aq, ak, av, bsz, window, dil)
        outs.append(o)
        lses.append(l)
    return combine_out(o_gla.reshape(m, GLA_V_W), outs, lses, w_out.astype(BF16), h)


def rglru_layer(h, bsz, gain, w_in, conv_w, conv_b, w_r, b_r, w_i, b_i, lam, w_out):
    m, d = h.shape
    t = m // bsz
    d_rnn = w_in.shape[1] // 2
    gate, xr = norm_matmul(h, gain, w_in.astype(BF16), [d_rnn, d_rnn], [F32, F32])
    y = lru(gate.reshape(bsz, t, d_rnn), xr.reshape(bsz, t, d_rnn), conv_w, conv_b, w_r, b_r, w_i, b_i, lam)
    return matmul_res(y.reshape(m, d_rnn), w_out.astype(BF16), h)


def kernel(x, mix_norm, ffn_norm, ab_w_in, gla_w_alpha, gla_b_alpha, gla_head_gain, ab_w_out,
           c_w_in, c_conv_w, c_conv_b, c_w_rgate, c_b_rgate, c_w_igate, c_b_igate, c_lambda, c_w_out,
           peer_w_query, peer_sub_keys, peer_u, peer_v, final_norm):
    bsz, t, d = x.shape
    depth = mix_norm.shape[0]
    h = x.reshape(bsz * t, d)
    for layer in range(depth):
        if layer % 2 == 0:
            e = layer // 2
            h = gla_dilated_layer(h, bsz, mix_norm[layer], ab_w_in[e], gla_w_alpha[e], gla_b_alpha[e],
                                  gla_head_gain[e], ab_w_out[e])
        else:
            o = layer // 2
            h = rglru_layer(h, bsz, mix_norm[layer], c_w_in[o], c_conv_w[o], c_conv_b[o], c_w_rgate[o],
                            c_b_rgate[o], c_w_igate[o], c_b_igate[o], c_lambda[o], c_w_out[o])
        h = peer_ffn(h, ffn_norm[layer], peer_w_query[layer], peer_sub_keys[layer], peer_u[layer],
                     peer_v[layer], final_norm, final_norm=(layer == depth - 1))
    return h.reshape(bsz, t, d)
```

```python
import functools
import math

import jax
import jax.numpy as jnp
from jax import lax
from jax.experimental import pallas as pl
from jax.experimental.pallas import tpu as pltpu

F32 = jnp.float32
BF16 = jnp.bfloat16

NORM_EPS = 1e-6
V7X_VMEM_LIMIT = 56 * 1024 * 1024

GLA_HEADS = 4
GLA_DK = 64
GLA_DV = 128
GLA_GATE_RANK = 16
GLA_GATE_TAU = 16.0
GLA_CHUNK = 64
ATT_HEADS = 8
ATT_HD = 64
DILATED_BRANCHES = ((128, 1), (512, 4), (2048, 16))
GLA_QK_W = GLA_HEADS * GLA_DK
GLA_V_W = GLA_HEADS * GLA_DV
ATT_W = ATT_HEADS * ATT_HD
CONV_W = 4
LRU_C = 8.0
RNN_BLOCK_W = 128
PEER_HEADS = 8
PEER_NKEYS = 128
PEER_TOPK = 16
PEER_SLOTS = PEER_HEADS * PEER_TOPK
LANES = 128
SUBLANES = 8

_NT = (((1,), (1,)), ((), ()))
_TN = (((0,), (0,)), ((), ()))


def _params(*sem):
    return pltpu.CompilerParams(dimension_semantics=sem, vmem_limit_bytes=V7X_VMEM_LIMIT)


def _gelu_tanh(x):
    c = math.sqrt(2.0 / math.pi)
    return (0.5 * x) * (1.0 + jnp.tanh(x * (c + (0.044715 * c) * (x * x))))


def _rms(x, gain):
    ms = jnp.mean(x * x, axis=-1, keepdims=True)
    return x * lax.rsqrt(ms + NORM_EPS) * gain


def _norm_matmul_kernel(x_ref, g_ref, w_ref, *o_refs, widths):
    xn = _rms(x_ref[...], g_ref[...]).astype(BF16)
    off = 0
    for o_ref, w in zip(o_refs, widths):
        y = jnp.dot(xn, w_ref[:, off:off + w], preferred_element_type=F32).astype(o_ref.dtype)
        if len(o_ref.shape) == 3:
            for c in range(w // LANES):
                o_ref[c] = y[:, c * LANES:(c + 1) * LANES]
        else:
            o_ref[...] = y
        off += w


def norm_matmul(x, gain, w_bf16, widths, dtypes, tm=512, lane_major=()):
    m, d = x.shape
    n = w_bf16.shape[1]
    assert sum(widths) == n and m % tm == 0
    out_specs, out_shape = [], []
    for k, (w, dt) in enumerate(zip(widths, dtypes)):
        if k in lane_major:
            out_specs.append(pl.BlockSpec((w // LANES, tm, LANES), lambda i: (0, i, 0)))
            out_shape.append(jax.ShapeDtypeStruct((w // LANES, m, LANES), dt))
        else:
            out_specs.append(pl.BlockSpec((tm, w), lambda i: (i, 0)))
            out_shape.append(jax.ShapeDtypeStruct((m, w), dt))
    return pl.pallas_call(
        functools.partial(_norm_matmul_kernel, widths=tuple(widths)),
        grid=(m // tm,),
        in_specs=[pl.BlockSpec((tm, d), lambda i: (i, 0)),
                  pl.BlockSpec((1, d), lambda i: (0, 0)),
                  pl.BlockSpec((d, n), lambda i: (0, 0))],
        out_specs=out_specs,
        out_shape=out_shape,
        compiler_params=_params("parallel"),
        name="norm_matmul",
    )(x, gain.reshape(1, d), w_bf16)


def _matmul_res_kernel(x_ref, w_ref, r_ref, o_ref):
    o_ref[...] = r_ref[...] + jnp.dot(x_ref[...].astype(BF16), w_ref[...], preferred_element_type=F32)


def matmul_res(x, w_bf16, res, tm=512):
    m, k = x.shape
    n = w_bf16.shape[1]
    return pl.pallas_call(
        _matmul_res_kernel,
        grid=(m // tm,),
        in_specs=[pl.BlockSpec((tm, k), lambda i: (i, 0)),
                  pl.BlockSpec((k, n), lambda i: (0, 0)),
                  pl.BlockSpec((tm, n), lambda i: (i, 0))],
        out_specs=pl.BlockSpec((tm, n), lambda i: (i, 0)),
        out_shape=jax.ShapeDtypeStruct((m, n), F32),
        compiler_params=_params("parallel"),
        name="matmul_res",
    )(x, w_bf16, res)


def _gla_kernel(q_ref, k_ref, v_ref, gg_ref, lr_ref, wa_ref, ba_ref, hg_ref, o_ref, st_ref, *, tg):
    c = GLA_CHUNK

    @pl.when(pl.program_id(1) == 0)
    def _():
        st_ref[...] = jnp.zeros_like(st_ref)

    z = jnp.dot(lr_ref[0], wa_ref[...], preferred_element_type=F32,
                precision=lax.Precision.HIGHEST) + ba_ref[...]
    log_a = (jnp.minimum(z, 0.0) - jnp.log1p(jnp.exp(-jnp.abs(z)))) * (1.0 / GLA_GATE_TAU)
    row = lax.broadcasted_iota(jnp.int32, (c, c), 0)
    col = lax.broadcasted_iota(jnp.int32, (c, c), 1)
    causal = row >= col
    tril = causal.astype(F32)
    for ci in range(tg // c):
        sl = slice(ci * c, (ci + 1) * c)
        b = jnp.dot(tril, log_a[sl], preferred_element_type=F32, precision=lax.Precision.HIGHEST)
        b_last = b[c - 1:c, :]
        eb = jnp.exp(b)
        q_dec = q_ref[0, sl, :] * (GLA_DK ** -0.5) * eb
        kk = k_ref[0, sl, :]
        k_inv = kk * jnp.exp(-b)
        k_end = kk * jnp.exp(b_last - b)
        decay = jnp.exp(b_last)
        vv = v_ref[0, sl, :]
        outs = []
        for h in range(GLA_HEADS):
            ks = slice(h * GLA_DK, (h + 1) * GLA_DK)
            vs = slice(h * GLA_DV, (h + 1) * GLA_DV)
            qh = q_dec[:, ks].astype(BF16)
            vh = vv[:, vs].astype(BF16)
            att = lax.dot_general(qh, k_inv[:, ks].astype(BF16), _NT, preferred_element_type=F32)
            att = jnp.where(causal, att, 0.0)
            st = st_ref[h]
            o_h = jnp.dot(att.astype(BF16), vh, preferred_element_type=F32)
            o_h = o_h + lax.dot_general(qh, st.astype(BF16), _NT, preferred_element_type=F32)
            kv_t = lax.dot_general(vh, k_end[:, ks].astype(BF16), _TN, preferred_element_type=F32)
            st_ref[h] = decay[:, ks] * st + kv_t
            o_h = o_h * lax.rsqrt(jnp.mean(o_h * o_h, axis=-1, keepdims=True) + NORM_EPS)
            outs.append(o_h)
        o = jnp.concatenate(outs, axis=-1)
        gg = gg_ref[0, sl, :]
        o_ref[0, sl, :] = o * hg_ref[...] * (gg * jax.nn.sigmoid(gg))


def gla(gq, gk, gv, gg, glr, w_alpha, b_alpha, head_gain, tg=256):
    bsz, t, _ = gq.shape
    wa = jnp.zeros((LANES, GLA_QK_W), F32).at[:GLA_GATE_RANK].set(w_alpha)
    spec = lambda w: pl.BlockSpec((1, tg, w), lambda b, i: (b, i, 0))
    const = lambda r, w: pl.BlockSpec((r, w), lambda b, i: (0, 0))
    return pl.pallas_call(
        functools.partial(_gla_kernel, tg=tg),
        grid=(bsz, t // tg),
        in_specs=[spec(GLA_QK_W), spec(GLA_QK_W), spec(GLA_V_W), spec(GLA_V_W), spec(LANES),
                  const(LANES, GLA_QK_W), const(1, GLA_QK_W), const(1, GLA_V_W)],
        out_specs=spec(GLA_V_W),
        out_shape=jax.ShapeDtypeStruct((bsz, t, GLA_V_W), F32),
        scratch_shapes=[pltpu.VMEM((GLA_HEADS, GLA_DV, GLA_DK), F32)],
        compiler_params=_params("parallel", "arbitrary"),
        name="gla",
    )(gq, gk, gv, gg, glr, wa, b_alpha.reshape(1, -1), head_gain.reshape(1, -1))


def _dilated_kernel(q_ref, kp_ref, kc_ref, vp_ref, vc_ref, o_ref, l_ref, *, dil, blk):
    i = pl.program_id(1)
    cblk = pl.program_id(2)
    n_pairs = q_ref.shape[0]
    n_cblk = ATT_HEADS // 2 // n_pairs
    qi = lax.broadcasted_iota(jnp.int32, (blk, 2 * blk), 0)
    kj = lax.broadcasted_iota(jnp.int32, (blk, 2 * blk), 1)
    delta = qi + blk - kj
    kj_min = jnp.where(i > 0, 0, blk)
    valid = (delta >= 0) & (delta <= blk) & (kj >= kj_min)
    dist = (delta * dil).astype(F32)
    lane = lax.broadcasted_iota(jnp.int32, (1, 2 * ATT_HD), 1)
    lo = lane < ATT_HD

    def alibi(lp, e):
        slopes = [2.0 ** (-8.0 * (2 * (c * n_pairs + lp) + e + 1) / ATT_HEADS) for c in range(n_cblk)]
        slope = jnp.float32(slopes[-1])
        for c in range(n_cblk - 2, -1, -1):
            slope = jnp.where(cblk == c, jnp.float32(slopes[c]), slope)
        return jnp.where(valid, -slope * dist, -1e30)

    biases = [[alibi(lp, e) for e in range(2)] for lp in range(n_pairs)]

    def body(r, carry):
        rows = pl.ds(r, blk, stride=dil)
        for lp in range(n_pairs):
            q2 = q_ref[lp, rows, :].astype(BF16)
            k2 = jnp.concatenate([kp_ref[lp, rows, :], kc_ref[lp, rows, :]], axis=0).astype(BF16)
            v2 = jnp.concatenate([vp_ref[lp, rows, :], vc_ref[lp, rows, :]], axis=0).astype(BF16)
            o_pair = jnp.zeros((blk, 2 * ATT_HD), F32)
            lse_pair = jnp.zeros((blk, 2 * ATT_HD), F32)
            for e in range(2):
                sel = lo if e == 0 else jnp.logical_not(lo)
                qm = jnp.where(sel, q2, jnp.zeros_like(q2))
                vm = jnp.where(sel, v2, jnp.zeros_like(v2))
                s = lax.dot_general(qm, k2, _NT, preferred_element_type=F32) * (ATT_HD ** -0.5)
                s = s + biases[lp][e]
                m = jnp.max(s, axis=-1, keepdims=True)
                pexp = jnp.exp(s - m)
                l = jnp.sum(pexp, axis=-1, keepdims=True)
                o_e = jnp.dot(pexp.astype(BF16), vm, preferred_element_type=F32)
                o_pair = o_pair + o_e * (1.0 / l)
                lse_pair = jnp.where(sel, m + jnp.log(l), lse_pair)
            o_ref[lp, rows, :] = o_pair
            l_ref[lp, rows, :] = lse_pair
        return carry

    if dil == 1:
        body(0, 0)
    else:
        lax.fori_loop(0, dil, body, 0)


def dilated_branch(q, k, v, bsz, window, dil):
    n_slab, m, pw = q.shape
    t = m // bsz
    blk = window // dil
    rows = blk * dil
    assert t % rows == 0 and pw == 2 * ATT_HD
    nblk = t // rows
    n_pairs = n_slab if rows <= 512 else n_slab // 2
    cur = pl.BlockSpec((n_pairs, rows, pw), lambda b, i, c: (c, b * nblk + i, 0))
    prev = pl.BlockSpec((n_pairs, rows, pw), lambda b, i, c: (c, b * nblk + jnp.maximum(i - 1, 0), 0))
    return pl.pallas_call(
        functools.partial(_dilated_kernel, dil=dil, blk=blk),
        grid=(bsz, nblk, n_slab // n_pairs),
        in_specs=[cur, prev, cur, prev, cur],
        out_specs=[cur, cur],
        out_shape=[jax.ShapeDtypeStruct((n_slab, m, pw), F32)] * 2,
        compiler_params=_params("parallel", "arbitrary", "arbitrary"),
        name=f"dilated_d{dil}",
    )(q, k, k, v, v)


def _combine_out_kernel(og_ref, o1_ref, o2_ref, o3_ref, l1_ref, l2_ref, l3_ref, w_ref, r_ref, out_ref):
    wide = lambda ref: jnp.concatenate([ref[c] for c in range(ref.shape[0])], axis=-1)
    l1, l2, l3 = wide(l1_ref), wide(l2_ref), wide(l3_ref)
    mx = jnp.maximum(jnp.maximum(l1, l2), l3)
    e1, e2, e3 = jnp.exp(l1 - mx), jnp.exp(l2 - mx), jnp.exp(l3 - mx)
    att = (e1 * wide(o1_ref) + e2 * wide(o2_ref) + e3 * wide(o3_ref)) / (e1 + e2 + e3)
    kg = og_ref.shape[1]
    acc = jnp.dot(og_ref[...].astype(BF16), w_ref[:kg, :], preferred_element_type=F32)
    acc = acc + jnp.dot(att.astype(BF16), w_ref[kg:, :], preferred_element_type=F32)
    out_ref[...] = r_ref[...] + acc


def combine_out(o_gla, outs, lses, w_bf16, res, tm=512):
    m = res.shape[0]
    n = w_bf16.shape[1]
    row = lambda w: pl.BlockSpec((tm, w), lambda i: (i, 0))
    slab = pl.BlockSpec((ATT_W // LANES, tm, LANES), lambda i: (0, i, 0))
    return pl.pallas_call(
        _combine_out_kernel,
        grid=(m // tm,),
        in_specs=[row(GLA_V_W)] + [slab] * 6
                 + [pl.BlockSpec(w_bf16.shape, lambda i: (0, 0)), row(n)],
        out_specs=row(n),
        out_shape=jax.ShapeDtypeStruct((m, n), F32),
        compiler_params=_params("parallel"),
        name="combine_out",
    )(o_gla, *outs, *lses, w_bf16, res)


def _lru_kernel(gate_ref, xr_ref, cw_ref, cb_ref, wr_ref, br_ref, wi_ref, bi_ref, lam_ref, y_ref,
                ext_ref, h_ref, *, tt):
    @pl.when(pl.program_id(1) == 0)
    def _():
        ext_ref[:SUBLANES, :] = jnp.zeros((SUBLANES, ext_ref.shape[1]), F32)
        h_ref[...] = jnp.zeros_like(h_ref)

    xr = xr_ref[0]
    d = xr.shape[1]
    ext_ref[SUBLANES:, :] = xr
    xc = cb_ref[...] + cw_ref[CONV_W - 1:CONV_W, :] * xr
    for j in range(CONV_W - 1):
        off = SUBLANES - (CONV_W - 1) + j
        xc = xc + cw_ref[j:j + 1, :] * ext_ref[off:off + tt, :]
    ext_ref[:SUBLANES, :] = xr[tt - SUBLANES:, :]
    xcb = xc.astype(BF16)
    rs, is_ = [], []
    for g in range(d // RNN_BLOCK_W):
        cs = slice(g * RNN_BLOCK_W, (g + 1) * RNN_BLOCK_W)
        rs.append(jnp.dot(xcb[:, cs], wr_ref[g], preferred_element_type=F32))
        is_.append(jnp.dot(xcb[:, cs], wi_ref[g], preferred_element_type=F32))
    r = jax.nn.sigmoid(jnp.concatenate(rs, axis=-1) + br_ref[...])
    ig = jax.nn.sigmoid(jnp.concatenate(is_, axis=-1) + bi_ref[...])
    lam = lam_ref[...]
    softplus = jnp.maximum(-lam, 0.0) + jnp.log1p(jnp.exp(-jnp.abs(lam)))
    log_a = (-LRU_C) * r * softplus
    a = jnp.exp(log_a)
    th = jnp.tanh(log_a)
    n = -2.0 * th
    u = jnp.where(n > 0.0, n * lax.rsqrt(n), 0.0) * lax.rsqrt(1.0 - th) * (ig * xc)
    a = a.reshape(tt // SUBLANES, SUBLANES, d)
    u = u.reshape(tt // SUBLANES, SUBLANES, d)
    rows = lax.broadcasted_iota(jnp.int32, (1, SUBLANES, 1), 1)
    s = 1
    while s < SUBLANES:
        keep = rows >= s
        a_sh = jnp.where(keep, pltpu.roll(a, s, axis=1), 1.0)
        u_sh = jnp.where(keep, pltpu.roll(u, s, axis=1), 0.0)
        u = u + a * u_sh
        a = a * a_sh
        s *= 2
    carry = h_ref[...]
    gate = gate_ref[0]
    for g in range(tt // SUBLANES):
        hs = u[g] + a[g] * carry
        carry = hs[SUBLANES - 1:SUBLANES, :]
        y_ref[0, g * SUBLANES:(g + 1) * SUBLANES, :] = _gelu_tanh(gate[g * SUBLANES:(g + 1) * SUBLANES]) * hs
    h_ref[...] = carry


def lru(gate, xr, conv_w, conv_b, w_r, b_r, w_i, b_i, lam, tt=256):
    bsz, t, d = xr.shape
    spec = pl.BlockSpec((1, tt, d), lambda b, i: (b, i, 0))
    row = pl.BlockSpec((1, d), lambda b, i: (0, 0))
    wspec = pl.BlockSpec(w_r.shape, lambda b, i: (0, 0, 0))
    return pl.pallas_call(
        functools.partial(_lru_kernel, tt=tt),
        grid=(bsz, t // tt),
        in_specs=[spec, spec, pl.BlockSpec((CONV_W, d), lambda b, i: (0, 0)), row,
                  wspec, row, wspec, row, row],
        out_specs=spec,
        out_shape=jax.ShapeDtypeStruct((bsz, t, d), F32),
        scratch_shapes=[pltpu.VMEM((tt + SUBLANES, d), F32), pltpu.VMEM((1, d), F32)],
        compiler_params=_params("parallel", "arbitrary"),
        name="lru",
    )(gate, xr, conv_w, conv_b.reshape(1, d), w_r.astype(BF16), b_r.reshape(1, d),
      w_i.astype(BF16), b_i.reshape(1, d), lam.reshape(1, d))


def _extract_top(vals, pos, n_rounds, payload=None):
    return _extract_top_many([(vals, pos, payload)], n_rounds)[0]


def _extract_top_many(problems, n_rounds):
    state, outs = [], []
    for vals, pos, payload in problems:
        lo_wins = vals[0] >= vals[1]
        win = lambda x, lo_wins=lo_wins: jnp.where(lo_wins, x[0], x[1])
        lose = lambda x, lo_wins=lo_wins: jnp.where(lo_wins, x[1], x[0])
        st = dict(v_w=win(vals), v_l=lose(vals), p_w=win(pos), p_l=lose(pos))
        if payload is not None:
            st.update(x_w=win(payload), x_l=lose(payload))
        state.append(st)
        outs.append(([], [], []))
    for _ in range(n_rounds):
        for st, (out_v, out_p, out_x) in zip(state, outs):
            m = jnp.max(st["v_w"], axis=0, keepdims=True)
            p = jnp.min(jnp.where(st["v_w"] == m, st["p_w"], 1e9), axis=0, keepdims=True)
            first = st["p_w"] == p
            if "x_w" in st:
                out_x.append(jnp.max(jnp.where(first, st["x_w"], -1.0), axis=0, keepdims=True))
                st["x_w"] = jnp.where(first, st["x_l"], st["x_w"])
            st["v_w"] = jnp.where(first, st["v_l"], st["v_w"])
            st["p_w"] = jnp.where(first, st["p_l"], st["p_w"])
            st["v_l"] = jnp.where(first, -jnp.inf, st["v_l"])
            out_v.append(m)
            out_p.append(p)
    return outs


def _peer_topk_kernel(q_ref, keys_ref, ids_ref, g_ref):
    k = PEER_TOPK
    tm = q_ref.shape[0]
    rows = lambda n: lax.broadcasted_iota(jnp.int32, (n, tm), 0).astype(F32)
    half = PEER_NKEYS // 2
    key_pos = (rows(half), rows(half) + half)
    halves = ([(0, 16), (1, 8), (2, 8), (3, 8)], [(4, 8), (5, 8), (6, 8), (7, 8)])
    id_rows, g_rows = [], []
    for h in range(PEER_HEADS):
        scores = []
        for p in range(2):
            keys = keys_ref[h, p]
            qs = q_ref[:, (2 * h + p) * LANES:(2 * h + p + 1) * LANES]
            s_t = tuple(lax.dot_general(keys[kh * half:(kh + 1) * half], qs, _NT, preferred_element_type=F32,
                                        precision=lax.Precision.HIGHEST) for kh in range(2))
            scores.append((s_t, key_pos, None))
        tops = [(jnp.concatenate(v, axis=0), jnp.concatenate(pidx, axis=0))
                for v, pidx, _ in _extract_top_many(scores, k)]
        (t1, i1), (t2, i2) = tops
        cand_v, cand_i, cand_p = [], [], []
        for hi, groups in enumerate(halves):
            cv = [t1[a:a + 1, :] + t2[0:nb, :] for a, nb in groups]
            ci = [i1[a:a + 1, :] * PEER_NKEYS + i2[0:nb, :] for a, nb in groups]
            cp = [a * k + rows(nb) for a, nb in groups]
            if hi == 1:
                cv.append(t1[8:16, :] + t2[0:1, :])
                ci.append(i1[8:16, :] * PEER_NKEYS + i2[0:1, :])
                cp.append((8.0 + rows(8)) * k)
            cand_v.append(jnp.concatenate(cv, axis=0))
            cand_i.append(jnp.concatenate(ci, axis=0))
            cand_p.append(jnp.concatenate(cp, axis=0))
        bv, _, bi = _extract_top(cand_v, cand_p, k, payload=cand_i)
        best = jnp.concatenate(bv, axis=0)
        e = jnp.exp(best - best[0:1, :])
        g_rows.append(e / jnp.sum(e, axis=0, keepdims=True))
        id_rows.append(jnp.concatenate(bi, axis=0))
    ids_t = jnp.concatenate(id_rows, axis=0)
    g_t = jnp.concatenate(g_rows, axis=0)
    for c in range(tm // LANES):
        cs = slice(c * LANES, (c + 1) * LANES)
        ids_ref[cs, :] = ids_t[:, cs].T.astype(jnp.int32)
        g_ref[cs, :] = g_t[:, cs].T


def peer_topk(q, sub_keys, tm=256):
    m = q.shape[0]
    return pl.pallas_call(
        _peer_topk_kernel,
        grid=(m // tm,),
        in_specs=[pl.BlockSpec((tm, q.shape[1]), lambda i: (i, 0)),
                  pl.BlockSpec(sub_keys.shape, lambda i: (0, 0, 0, 0))],
        out_specs=[pl.BlockSpec((tm, PEER_SLOTS), lambda i: (i, 0))] * 2,
        out_shape=[jax.ShapeDtypeStruct((m, PEER_SLOTS), jnp.int32),
                   jax.ShapeDtypeStruct((m, PEER_SLOTS), F32)],
        compiler_params=_params("parallel"),
        name="peer_topk",
    )(q, sub_keys)


def _transpose_sublane_blocks(x):
    x = list(x)
    sub = lax.broadcasted_iota(jnp.int32, x[0].shape, 0)
    s = SUBLANES // 2
    while s >= 1:
        clear = (sub & s) == 0
        for i in range(SUBLANES):
            if i & s:
                continue
            a, b = x[i], x[i + s]
            x[i] = jnp.where(clear, a, pltpu.roll(b, s, axis=0))
            x[i + s] = jnp.where(clear, pltpu.roll(a, SUBLANES - s, axis=0), b)
        s //= 2
    return x


PEER_GROUP = 16


def _peer_onehot_kernel(ids_ref, g_ref, w_ref, ra_ref, rb_ref):
    tm = ids_ref.shape[0]
    n_grp = tm // PEER_GROUP
    sub = lax.broadcasted_iota(jnp.int32, (PEER_NKEYS, PEER_SLOTS), 0)
    zeros = jnp.zeros((PEER_NKEYS, PEER_SLOTS), F32)

    def maps(grp, r_ref):
        m0 = pl.multiple_of(grp * PEER_GROUP, PEER_GROUP)
        for t in range(0, PEER_GROUP, 2):
            gates, hots = [], []
            for tok in (t, t + 1):
                ids = ids_ref[pl.ds(m0 + tok, 1), :]
                gates.append(jnp.where(sub == (ids >> 7), g_ref[pl.ds(m0 + tok, 1), :], 0.0))
                hots.append(jnp.where(sub == (ids & (PEER_NKEYS - 1)), 1.0, 0.0))
            lhs = jnp.concatenate([jnp.concatenate([gates[0], zeros], axis=1),
                                   jnp.concatenate([zeros, gates[1]], axis=1)], axis=0)
            bt = jnp.concatenate(hots, axis=1)
            r = lax.dot_general(lhs.astype(BF16), bt.astype(BF16), _NT, preferred_element_type=F32)
            r_ref[t] = r[:PEER_NKEYS]
            r_ref[t + 1] = r[PEER_NKEYS:]

    def emit(grp, r_ref):
        m0 = pl.multiple_of(grp * PEER_GROUP, PEER_GROUP)
        for gi in range(PEER_NKEYS // SUBLANES):
            rows = slice(gi * SUBLANES, (gi + 1) * SUBLANES)
            ya = _transpose_sublane_blocks([r_ref[t, rows, :] for t in range(SUBLANES)])
            yb = _transpose_sublane_blocks([r_ref[SUBLANES + t, rows, :] for t in range(SUBLANES)])
            for r in range(SUBLANES):
                tile = jnp.concatenate([ya[r], yb[r]], axis=0).astype(BF16)
                w_ref[gi * SUBLANES + r, pl.ds(m0, PEER_GROUP), :] = tile

    maps(0, ra_ref)

    def body(k, carry):
        g0 = 2 * k
        maps(g0 + 1, rb_ref)
        emit(g0, ra_ref)
        maps(g0 + 2, ra_ref)
        emit(g0 + 1, rb_ref)
        return carry

    lax.fori_loop(0, n_grp // 2 - 1, body, 0)
    maps(n_grp - 1, rb_ref)
    emit(n_grp - 2, ra_ref)
    emit(n_grp - 1, rb_ref)


def peer_onehot(ids, g, tm=256):
    m = ids.shape[0]
    assert tm % (2 * PEER_GROUP) == 0
    res = pltpu.VMEM((PEER_GROUP, PEER_NKEYS, PEER_NKEYS), F32)
    return pl.pallas_call(
        _peer_onehot_kernel,
        grid=(m // tm,),
        in_specs=[pl.BlockSpec((tm, PEER_SLOTS), lambda i: (i, 0))] * 2,
        out_specs=pl.BlockSpec((PEER_NKEYS, tm, PEER_NKEYS), lambda i: (0, i, 0)),
        out_shape=jax.ShapeDtypeStruct((PEER_NKEYS, m, PEER_NKEYS), BF16),
        scratch_shapes=[res, res],
        compiler_params=_params("parallel"),
        name="peer_onehot",
    )(ids, g)


def _peer_dense_kernel(h_ref, gain_ref, w_ref, u_ref, v_ref, fg_ref, o_ref, hn_ref, acc_ref, *,
                       groups, final_norm):
    j = pl.program_id(1)

    @pl.when(j == 0)
    def _():
        hn_ref[...] = _rms(h_ref[...], gain_ref[...]).astype(BF16)
        acc_ref[...] = jnp.zeros_like(acc_ref)

    z = jnp.dot(hn_ref[...], u_ref[...], preferred_element_type=F32)
    wsel = jnp.concatenate([w_ref[gi] for gi in range(groups)], axis=-1)
    act = _gelu_tanh(z).astype(BF16) * wsel
    acc_ref[...] += jnp.dot(act, v_ref[...], preferred_element_type=F32)

    @pl.when(j == pl.num_programs(1) - 1)
    def _():
        out = h_ref[...] + acc_ref[...]
        if final_norm:
            out = _rms(out, fg_ref[...])
        o_ref[...] = out


def peer_dense(h, gain, wmap, ut_bf16, v_bf16, final_gain, final_norm, tm=512, groups=8):
    m, d = h.shape
    ne = v_bf16.shape[0]
    te = groups * PEER_NKEYS
    return pl.pallas_call(
        functools.partial(_peer_dense_kernel, groups=groups, final_norm=final_norm),
        grid=(m // tm, ne // te),
        in_specs=[pl.BlockSpec((tm, d), lambda i, j: (i, 0)),
                  pl.BlockSpec((1, d), lambda i, j: (0, 0)),
                  pl.BlockSpec((groups, tm, PEER_NKEYS), lambda i, j: (j, i, 0)),
                  pl.BlockSpec((d, te), lambda i, j: (0, j)),
                  pl.BlockSpec((te, d), lambda i, j: (j, 0)),
                  pl.BlockSpec((1, d), lambda i, j: (0, 0))],
        out_specs=pl.BlockSpec((tm, d), lambda i, j: (i, 0)),
        out_shape=jax.ShapeDtypeStruct((m, d), F32),
        scratch_shapes=[pltpu.VMEM((tm, d), BF16), pltpu.VMEM((tm, d), F32)],
        compiler_params=_params("parallel", "arbitrary"),
        name="peer_dense",
    )(h, gain.reshape(1, d), wmap, ut_bf16, v_bf16, final_gain.reshape(1, d))


def peer_ffn(h, gain, w_query, sub_keys, u_tab, v_tab, final_gain, final_norm):
    (q,) = norm_matmul(h, gain, w_query.astype(BF16), [w_query.shape[1]], [F32])
    ids, g = peer_topk(q, sub_keys)
    wmap = peer_onehot(ids, g)
    return peer_dense(h, gain, wmap, u_tab.astype(BF16).T, v_tab.astype(BF16), final_gain, final_norm)


def gla_dilated_layer(h, bsz, gain, w_in, w_alpha, b_alpha, head_gain, w_out):
    m, d = h.shape
    t = m // bsz
    sizes = [GLA_QK_W, GLA_QK_W, GLA_V_W, GLA_V_W, GLA_GATE_RANK, ATT_W, ATT_W, ATT_W]
    offs = [0]
    for s in sizes:
        offs.append(offs[-1] + s)
    seg = lambda i: w_in[:, offs[i]:offs[i + 1]]
    pad = jnp.zeros((d, LANES - GLA_GATE_RANK), w_in.dtype)
    w_cat = jnp.concatenate([seg(0), seg(1), seg(2), seg(3), seg(5), seg(6), seg(7), seg(4), pad], axis=1)
    widths = [GLA_QK_W, GLA_QK_W, GLA_V_W, GLA_V_W, ATT_W, ATT_W, ATT_W, LANES]
    dts = [F32] * len(widths)
    gq, gk, gv, gg, aq, ak, av, glr = norm_matmul(h, gain, w_cat.astype(BF16), widths, dts,
                                                  lane_major=(4, 5, 6))
    r3 = lambda a: a.reshape(bsz, t, a.shape[1])
    o_gla = gla(r3(gq), r3(gk), r3(gv), r3(gg), r3(glr), w_alpha, b_alpha, head_gain)
    outs, lses = [], []
    for window, dil in DILATED_BRANCHES:
        o, l = dilated_branch(aq, ak, av, bsz, window, dil)
        outs.append(o)
        lses.append(l)
    return combine_out(o_gla.reshape(m, GLA_V_W), outs, lses, w_out.astype(BF16), h)


def rglru_layer(h, bsz, gain, w_in, conv_w, conv_b, w_r, b_r, w_i, b_i, lam, w_out):
    m, d = h.shape
    t = m // bsz
    d_rnn = w_in.shape[1] // 2
    gate, xr = norm_matmul(h, gain, w_in.astype(BF16), [d_rnn, d_rnn], [F32, F32])
    y = lru(gate.reshape(bsz, t, d_rnn), xr.reshape(bsz, t, d_rnn), conv_w, conv_b, w_r, b_r, w_i, b_i, lam)
    return matmul_res(y.reshape(m, d_rnn), w_out.astype(BF16), h)


def kernel(x, mix_norm, ffn_norm, ab_w_in, gla_w_alpha, gla_b_alpha, gla_head_gain, ab_w_out,
           c_w_in, c_conv_w, c_conv_b, c_w_rgate, c_b_rgate, c_w_igate, c_b_igate, c_lambda, c_w_out,
           peer_w_query, peer_sub_keys, peer_u, peer_v, final_norm):
    bsz, t, d = x.shape
    depth = mix_norm.shape[0]
    h = x.reshape(bsz * t, d)
    for layer in range(depth):
        if layer % 2 == 0:
            e = layer // 2
            h = gla_dilated_layer(h, bsz, mix_norm[layer], ab_w_in[e], gla_w_alpha[e], gla_b_alpha[e],
                                  gla_head_gain[e], ab_w_out[e])
        else:
            o = layer // 2
            h = rglru_layer(h, bsz, mix_norm[layer], c_w_in[o], c_conv_w[o], c_conv_b[o], c_w_rgate[o],
                            c_b_rgate[o], c_w_igate[o], c_b_igate[o], c_lambda[o], c_w_out[o])
        h = peer_ffn(h, ffn_norm[layer], peer_w_query[layer], peer_sub_keys[layer], peer_u[layer],
                     peer_v[layer], final_norm, final_norm=(layer == depth - 1))
    return h.reshape(bsz, t, d)
```

```python
import functools
import math

import jax
import jax.numpy as jnp
from jax import lax
from jax.experimental import pallas as pl
from jax.experimental.pallas import tpu as pltpu

F32 = jnp.float32
BF16 = jnp.bfloat16

NORM_EPS = 1e-6
V7X_VMEM_LIMIT = 56 * 1024 * 1024

GLA_HEADS = 4
GLA_DK = 64
GLA_DV = 128
GLA_GATE_RANK = 16
GLA_GATE_TAU = 16.0
GLA_CHUNK = 64
ATT_HEADS = 8
ATT_HD = 64
DILATED_BRANCHES = ((128, 1), (512, 4), (2048, 16))
GLA_QK_W = GLA_HEADS * GLA_DK
GLA_V_W = GLA_HEADS * GLA_DV
ATT_W = ATT_HEADS * ATT_HD
CONV_W = 4
LRU_C = 8.0
RNN_BLOCK_W = 128
PEER_HEADS = 8
PEER_NKEYS = 128
PEER_TOPK = 16
PEER_SLOTS = PEER_HEADS * PEER_TOPK
LANES = 128
SUBLANES = 8

_NT = (((1,), (1,)), ((), ()))
_TN = (((0,), (0,)), ((), ()))


def _params(*sem):
    return pltpu.CompilerParams(dimension_semantics=sem, vmem_limit_bytes=V7X_VMEM_LIMIT)


def _gelu_tanh(x):
    c = math.sqrt(2.0 / math.pi)
    return (0.5 * x) * (1.0 + jnp.tanh(x * (c + (0.044715 * c) * (x * x))))


def _rms(x, gain):
    ms = jnp.mean(x * x, axis=-1, keepdims=True)
    return x * lax.rsqrt(ms + NORM_EPS) * gain


def _norm_matmul_kernel(x_ref, g_ref, w_ref, *o_refs, widths):
    xn = _rms(x_ref[...], g_ref[...]).astype(BF16)
    off = 0
    for o_ref, w in zip(o_refs, widths):
        y = jnp.dot(xn, w_ref[:, off:off + w], preferred_element_type=F32).astype(o_ref.dtype)
        if len(o_ref.shape) == 3:
            for c in range(w // LANES):
                o_ref[c] = y[:, c * LANES:(c + 1) * LANES]
        else:
            o_ref[...] = y
        off += w


def norm_matmul(x, gain, w_bf16, widths, dtypes, tm=512, lane_major=()):
    m, d = x.shape
    n = w_bf16.shape[1]
    assert sum(widths) == n and m % tm == 0
    out_specs, out_shape = [], []
    for k, (w, dt) in enumerate(zip(widths, dtypes)):
        if k in lane_major:
            out_specs.append(pl.BlockSpec((w // LANES, tm, LANES), lambda i: (0, i, 0)))
            out_shape.append(jax.ShapeDtypeStruct((w // LANES, m, LANES), dt))
        else:
            out_specs.append(pl.BlockSpec((tm, w), lambda i: (i, 0)))
            out_shape.append(jax.ShapeDtypeStruct((m, w), dt))
    return pl.pallas_call(
        functools.partial(_norm_matmul_kernel, widths=tuple(widths)),
        grid=(m // tm,),
        in_specs=[pl.BlockSpec((tm, d), lambda i: (i, 0)),
                  pl.BlockSpec((1, d), lambda i: (0, 0)),
                  pl.BlockSpec((d, n), lambda i: (0, 0))],
        out_specs=out_specs,
        out_shape=out_shape,
        compiler_params=_params("parallel"),
        name="norm_matmul",
    )(x, gain.reshape(1, d), w_bf16)


def _matmul_res_kernel(x_ref, w_ref, r_ref, o_ref):
    o_ref[...] = r_ref[...] + jnp.dot(x_ref[...].astype(BF16), w_ref[...], preferred_element_type=F32)


def matmul_res(x, w_bf16, res, tm=512):
    m, k = x.shape
    n = w_bf16.shape[1]
    return pl.pallas_call(
        _matmul_res_kernel,
        grid=(m // tm,),
        in_specs=[pl.BlockSpec((tm, k), lambda i: (i, 0)),
                  pl.BlockSpec((k, n), lambda i: (0, 0)),
                  pl.BlockSpec((tm, n), lambda i: (i, 0))],
        out_specs=pl.BlockSpec((tm, n), lambda i: (i, 0)),
        out_shape=jax.ShapeDtypeStruct((m, n), F32),
        compiler_params=_params("parallel"),
        name="matmul_res",
    )(x, w_bf16, res)


def _gla_kernel(q_ref, k_ref, v_ref, gg_ref, lr_ref, wa_ref, ba_ref, hg_ref, o_ref, st_ref, *, tg):
    c = GLA_CHUNK

    @pl.when(pl.program_id(1) == 0)
    def _():
        st_ref[...] = jnp.zeros_like(st_ref)

    z = jnp.dot(lr_ref[0], wa_ref[...], preferred_element_type=F32,
                precision=lax.Precision.HIGHEST) + ba_ref[...]
    log_a = (jnp.minimum(z, 0.0) - jnp.log1p(jnp.exp(-jnp.abs(z)))) * (1.0 / GLA_GATE_TAU)
    nh = GLA_HEADS
    tril = (lax.broadcasted_iota(jnp.int32, (c, c), 0) >= lax.broadcasted_iota(jnp.int32, (c, c), 1)).astype(F32)
    klane_head = lax.broadcasted_iota(jnp.int32, (1, GLA_QK_W), 1) // GLA_DK
    vlane_head = lax.broadcasted_iota(jnp.int32, (1, GLA_V_W), 1) // GLA_DV
    row_head = lax.broadcasted_iota(jnp.int32, (nh * c, 1), 0) // c
    row_t = lax.broadcasted_iota(jnp.int32, (nh * c, c), 0) % c
    causal_rows = row_t >= lax.broadcasted_iota(jnp.int32, (nh * c, c), 1)
    q_rows_mask = row_head == klane_head
    o_rows_mask = row_head == vlane_head
    st_mask = (lax.broadcasted_iota(jnp.int32, (GLA_V_W, 1), 0) // GLA_DV) == klane_head
    for ci in range(tg // c):
        sl = slice(ci * c, (ci + 1) * c)
        b = jnp.dot(tril, log_a[sl], preferred_element_type=F32, precision=lax.Precision.HIGHEST)
        b_last = b[c - 1:c, :]
        q_dec = q_ref[0, sl, :] * (GLA_DK ** -0.5) * jnp.exp(b)
        kk = k_ref[0, sl, :]
        k_inv = (kk * jnp.exp(-b)).astype(BF16)
        k_end = (kk * jnp.exp(b_last - b)).astype(BF16)
        decay = jnp.exp(b_last)
        vv = v_ref[0, sl, :].astype(BF16)
        qb = q_dec.astype(BF16)
        q_stack = jnp.where(q_rows_mask, jnp.concatenate([qb] * nh, axis=0), jnp.zeros((), BF16))
        att = lax.dot_general(q_stack, k_inv, _NT, preferred_element_type=F32)
        att = jnp.where(causal_rows, att, 0.0).astype(BF16)
        o_all = jnp.dot(att, vv, preferred_element_type=F32)
        o_all = jnp.where(o_rows_mask, o_all, 0.0)
        o = o_all[0:c]
        for h in range(1, nh):
            o = o + o_all[h * c:(h + 1) * c]
        st = st_ref[...]
        o = o + lax.dot_general(qb, st.astype(BF16), _NT, preferred_element_type=F32)
        kv_t = lax.dot_general(vv, k_end, _TN, preferred_element_type=F32)
        st_ref[...] = decay * st + jnp.where(st_mask, kv_t, 0.0)
        outs = []
        for h in range(nh):
            o_h = o[:, h * GLA_DV:(h + 1) * GLA_DV]
            outs.append(o_h * lax.rsqrt(jnp.mean(o_h * o_h, axis=-1, keepdims=True) + NORM_EPS))
        o = jnp.concatenate(outs, axis=-1)
        gg = gg_ref[0, sl, :]
        o_ref[0, sl, :] = o * hg_ref[...] * (gg * jax.nn.sigmoid(gg))


def gla(gq, gk, gv, gg, glr, w_alpha, b_alpha, head_gain, tg=256):
    bsz, t, _ = gq.shape
    wa = jnp.zeros((LANES, GLA_QK_W), F32).at[:GLA_GATE_RANK].set(w_alpha)
    spec = lambda w: pl.BlockSpec((1, tg, w), lambda b, i: (b, i, 0))
    const = lambda r, w: pl.BlockSpec((r, w), lambda b, i: (0, 0))
    return pl.pallas_call(
        functools.partial(_gla_kernel, tg=tg),
        grid=(bsz, t // tg),
        in_specs=[spec(GLA_QK_W), spec(GLA_QK_W), spec(GLA_V_W), spec(GLA_V_W), spec(LANES),
                  const(LANES, GLA_QK_W), const(1, GLA_QK_W), const(1, GLA_V_W)],
        out_specs=spec(GLA_V_W),
        out_shape=jax.ShapeDtypeStruct((bsz, t, GLA_V_W), F32),
        scratch_shapes=[pltpu.VMEM((GLA_V_W, GLA_QK_W), F32)],
        compiler_params=_params("parallel", "arbitrary"),
        name="gla",
    )(gq, gk, gv, gg, glr, wa, b_alpha.reshape(1, -1), head_gain.reshape(1, -1))


def _dilated_kernel(q_ref, kp_ref, kc_ref, vp_ref, vc_ref, o_ref, l_ref, *, dil, blk):
    i = pl.program_id(1)
    cblk = pl.program_id(2)
    n_pairs = q_ref.shape[0]
    n_cblk = ATT_HEADS // 2 // n_pairs
    qi = lax.broadcasted_iota(jnp.int32, (blk, 2 * blk), 0)
    kj = lax.broadcasted_iota(jnp.int32, (blk, 2 * blk), 1)
    delta = qi + blk - kj
    kj_min = jnp.where(i > 0, 0, blk)
    valid = (delta >= 0) & (delta <= blk) & (kj >= kj_min)
    dist = (delta * dil).astype(F32)
    lane = lax.broadcasted_iota(jnp.int32, (1, 2 * ATT_HD), 1)
    lo = lane < ATT_HD

    def alibi(lp, e):
        slopes = [2.0 ** (-8.0 * (2 * (c * n_pairs + lp) + e + 1) / ATT_HEADS) for c in range(n_cblk)]
        slope = jnp.float32(slopes[-1])
        for c in range(n_cblk - 2, -1, -1):
            slope = jnp.where(cblk == c, jnp.float32(slopes[c]), slope)
        return jnp.where(valid, -slope * dist, -1e30)

    biases = [[alibi(lp, e) for e in range(2)] for lp in range(n_pairs)]

    per_trip = max(1, min(dil, ATT_HEADS // (2 * n_pairs)))

    def body(trip, carry):
        for sub in range(per_trip):
            one_class(trip + sub * (dil // per_trip))
        return carry

    def one_class(r):
        rows = pl.ds(r, blk, stride=dil)
        for lp in range(n_pairs):
            q2 = q_ref[lp, rows, :].astype(BF16)
            k2 = jnp.concatenate([kp_ref[lp, rows, :], kc_ref[lp, rows, :]], axis=0).astype(BF16)
            v2 = jnp.concatenate([vp_ref[lp, rows, :], vc_ref[lp, rows, :]], axis=0).astype(BF16)
            o_pair = jnp.zeros((blk, 2 * ATT_HD), F32)
            lse_pair = jnp.zeros((blk, 2 * ATT_HD), F32)
            for e in range(2):
                sel = lo if e == 0 else jnp.logical_not(lo)
                qm = jnp.where(sel, q2, jnp.zeros_like(q2))
                vm = jnp.where(sel, v2, jnp.zeros_like(v2))
                s = lax.dot_general(qm, k2, _NT, preferred_element_type=F32) * (ATT_HD ** -0.5)
                s = s + biases[lp][e]
                m = jnp.max(s, axis=-1, keepdims=True)
                pexp = jnp.exp(s - m)
                l = jnp.sum(pexp, axis=-1, keepdims=True)
                o_e = jnp.dot(pexp.astype(BF16), vm, preferred_element_type=F32)
                o_pair = o_pair + o_e * (1.0 / l)
                lse_pair = jnp.where(sel, m + jnp.log(l), lse_pair)
            o_ref[lp, rows, :] = o_pair
            l_ref[lp, rows, :] = lse_pair

    if dil == per_trip:
        body(0, 0)
    else:
        lax.fori_loop(0, dil // per_trip, body, 0)


def dilated_branch(q, k, v, bsz, window, dil):
    n_slab, m, pw = q.shape
    t = m // bsz
    blk = window // dil
    rows = blk * dil
    assert t % rows == 0 and pw == 2 * ATT_HD
    nblk = t // rows
    n_pairs = n_slab if rows <= 512 else n_slab // 2
    cur = pl.BlockSpec((n_pairs, rows, pw), lambda b, i, c: (c, b * nblk + i, 0))
    prev = pl.BlockSpec((n_pairs, rows, pw), lambda b, i, c: (c, b * nblk + jnp.maximum(i - 1, 0), 0))
    return pl.pallas_call(
        functools.partial(_dilated_kernel, dil=dil, blk=blk),
        grid=(bsz, nblk, n_slab // n_pairs),
        in_specs=[cur, prev, cur, prev, cur],
        out_specs=[cur, cur],
        out_shape=[jax.ShapeDtypeStruct((n_slab, m, pw), F32)] * 2,
        compiler_params=_params("parallel", "arbitrary", "arbitrary"),
        name=f"dilated_d{dil}",
    )(q, k, k, v, v)


def _combine_out_kernel(og_ref, o1_ref, o2_ref, o3_ref, l1_ref, l2_ref, l3_ref, w_ref, r_ref, out_ref):
    wide = lambda ref: jnp.concatenate([ref[c] for c in range(ref.shape[0])], axis=-1)
    l1, l2, l3 = wide(l1_ref), wide(l2_ref), wide(l3_ref)
    mx = jnp.maximum(jnp.maximum(l1, l2), l3)
    e1, e2, e3 = jnp.exp(l1 - mx), jnp.exp(l2 - mx), jnp.exp(l3 - mx)
    att = (e1 * wide(o1_ref) + e2 * wide(o2_ref) + e3 * wide(o3_ref)) / (e1 + e2 + e3)
    kg = og_ref.shape[1]
    acc = jnp.dot(og_ref[...].astype(BF16), w_ref[:kg, :], preferred_element_type=F32)
    acc = acc + jnp.dot(att.astype(BF16), w_ref[kg:, :], preferred_element_type=F32)
    out_ref[...] = r_ref[...] + acc


def combine_out(o_gla, outs, lses, w_bf16, res, tm=512):
    m = res.shape[0]
    n = w_bf16.shape[1]
    row = lambda w: pl.BlockSpec((tm, w), lambda i: (i, 0))
    slab = pl.BlockSpec((ATT_W // LANES, tm, LANES), lambda i: (0, i, 0))
    return pl.pallas_call(
        _combine_out_kernel,
        grid=(m // tm,),
        in_specs=[row(GLA_V_W)] + [slab] * 6
                 + [pl.BlockSpec(w_bf16.shape, lambda i: (0, 0)), row(n)],
        out_specs=row(n),
        out_shape=jax.ShapeDtypeStruct((m, n), F32),
        compiler_params=_params("parallel"),
        name="combine_out",
    )(o_gla, *outs, *lses, w_bf16, res)


def _lru_kernel(gate_ref, xr_ref, cw_ref, cb_ref, wr_ref, br_ref, wi_ref, bi_ref, lam_ref, y_ref,
                ext_ref, h_ref, *, tt):
    @pl.when(pl.program_id(1) == 0)
    def _():
        ext_ref[:SUBLANES, :] = jnp.zeros((SUBLANES, ext_ref.shape[1]), F32)
        h_ref[...] = jnp.zeros_like(h_ref)

    xr = xr_ref[0]
    d = xr.shape[1]
    ext_ref[SUBLANES:, :] = xr
    xc = cb_ref[...] + cw_ref[CONV_W - 1:CONV_W, :] * xr
    for j in range(CONV_W - 1):
        off = SUBLANES - (CONV_W - 1) + j
        xc = xc + cw_ref[j:j + 1, :] * ext_ref[off:off + tt, :]
    ext_ref[:SUBLANES, :] = xr[tt - SUBLANES:, :]
    xcb = xc.astype(BF16)
    rs, is_ = [], []
    for g in range(d // RNN_BLOCK_W):
        cs = slice(g * RNN_BLOCK_W, (g + 1) * RNN_BLOCK_W)
        rs.append(jnp.dot(xcb[:, cs], wr_ref[g], preferred_element_type=F32))
        is_.append(jnp.dot(xcb[:, cs], wi_ref[g], preferred_element_type=F32))
    r = jax.nn.sigmoid(jnp.concatenate(rs, axis=-1) + br_ref[...])
    ig = jax.nn.sigmoid(jnp.concatenate(is_, axis=-1) + bi_ref[...])
    lam = lam_ref[...]
    softplus = jnp.maximum(-lam, 0.0) + jnp.log1p(jnp.exp(-jnp.abs(lam)))
    log_a = (-LRU_C) * r * softplus
    a = jnp.exp(log_a)
    th = jnp.tanh(log_a)
    n = -2.0 * th
    u = jnp.where(n > 0.0, n * lax.rsqrt(n), 0.0) * lax.rsqrt(1.0 - th) * (ig * xc)
    a = a.reshape(tt // SUBLANES, SUBLANES, d)
    u = u.reshape(tt // SUBLANES, SUBLANES, d)
    rows = lax.broadcasted_iota(jnp.int32, (1, SUBLANES, 1), 1)
    s = 1
    while s < SUBLANES:
        keep = rows >= s
        a_sh = jnp.where(keep, pltpu.roll(a, s, axis=1), 1.0)
        u_sh = jnp.where(keep, pltpu.roll(u, s, axis=1), 0.0)
        u = u + a * u_sh
        a = a * a_sh
        s *= 2
    carry = h_ref[...]
    gate = gate_ref[0]
    for g in range(tt // SUBLANES):
        hs = u[g] + a[g] * carry
        carry = hs[SUBLANES - 1:SUBLANES, :]
        y_ref[0, g * SUBLANES:(g + 1) * SUBLANES, :] = _gelu_tanh(gate[g * SUBLANES:(g + 1) * SUBLANES]) * hs
    h_ref[...] = carry


def lru(gate, xr, conv_w, conv_b, w_r, b_r, w_i, b_i, lam, tt=256):
    bsz, t, d = xr.shape
    spec = pl.BlockSpec((1, tt, d), lambda b, i: (b, i, 0))
    row = pl.BlockSpec((1, d), lambda b, i: (0, 0))
    wspec = pl.BlockSpec(w_r.shape, lambda b, i: (0, 0, 0))
    return pl.pallas_call(
        functools.partial(_lru_kernel, tt=tt),
        grid=(bsz, t // tt),
        in_specs=[spec, spec, pl.BlockSpec((CONV_W, d), lambda b, i: (0, 0)), row,
                  wspec, row, wspec, row, row],
        out_specs=spec,
        out_shape=jax.ShapeDtypeStruct((bsz, t, d), F32),
        scratch_shapes=[pltpu.VMEM((tt + SUBLANES, d), F32), pltpu.VMEM((1, d), F32)],
        compiler_params=_params("parallel", "arbitrary"),
        name="lru",
    )(gate, xr, conv_w, conv_b.reshape(1, d), w_r.astype(BF16), b_r.reshape(1, d),
      w_i.astype(BF16), b_i.reshape(1, d), lam.reshape(1, d))


def _extract_top(vals, pos, n_rounds, payload=None):
    return _extract_top_many([(vals, pos, payload)], n_rounds)[0]


def _extract_top_many(problems, n_rounds):
    state, outs = [], []
    for vals, pos, payload in problems:
        lo_wins = vals[0] >= vals[1]
        win = lambda x, lo_wins=lo_wins: jnp.where(lo_wins, x[0], x[1])
        lose = lambda x, lo_wins=lo_wins: jnp.where(lo_wins, x[1], x[0])
        st = dict(v_w=win(vals), v_l=lose(vals), p_w=win(pos), p_l=lose(pos))
        if payload is not None:
            st.update(x_w=win(payload), x_l=lose(payload))
        state.append(st)
        outs.append(([], [], []))
    for _ in range(n_rounds):
        for st, (out_v, out_p, out_x) in zip(state, outs):
            m = jnp.max(st["v_w"], axis=0, keepdims=True)
            p = jnp.min(jnp.where(st["v_w"] == m, st["p_w"], 1e9), axis=0, keepdims=True)
            first = st["p_w"] == p
            if "x_w" in st:
                out_x.append(jnp.max(jnp.where(first, st["x_w"], -1.0), axis=0, keepdims=True))
                st["x_w"] = jnp.where(first, st["x_l"], st["x_w"])
            st["v_w"] = jnp.where(first, st["v_l"], st["v_w"])
            st["p_w"] = jnp.where(first, st["p_l"], st["p_w"])
            st["v_l"] = jnp.where(first, -jnp.inf, st["v_l"])
            out_v.append(m)
            out_p.append(p)
    return outs


def _topk_heads(q_ref, keys_ref):
    k = PEER_TOPK
    tm = q_ref.shape[0]
    n_heads = keys_ref.shape[0]
    rows = lambda n: lax.broadcasted_iota(jnp.int32, (n, tm), 0).astype(F32)
    half = PEER_NKEYS // 2
    key_pos = (rows(half), rows(half) + half)
    halves = ([(0, 16), (1, 8), (2, 8), (3, 8)], [(4, 8), (5, 8), (6, 8), (7, 8)])
    id_rows, g_rows = [], []
    for h in range(n_heads):
        scores = []
        for p in range(2):
            keys = keys_ref[h, p]
            qs = q_ref[:, (2 * h + p) * LANES:(2 * h + p + 1) * LANES]
            s_t = tuple(lax.dot_general(keys[kh * half:(kh + 1) * half], qs, _NT, preferred_element_type=F32,
                                        precision=lax.Precision.HIGHEST) for kh in range(2))
            scores.append((s_t, key_pos, None))
        tops = [(jnp.concatenate(v, axis=0), jnp.concatenate(pidx, axis=0))
                for v, pidx, _ in _extract_top_many(scores, k)]
        (t1, i1), (t2, i2) = tops
        cand_v, cand_i, cand_p = [], [], []
        for hi, groups in enumerate(halves):
            cv = [t1[a:a + 1, :] + t2[0:nb, :] for a, nb in groups]
            ci = [i1[a:a + 1, :] * PEER_NKEYS + i2[0:nb, :] for a, nb in groups]
            cp = [a * k + rows(nb) for a, nb in groups]
            if hi == 1:
                cv.append(t1[8:16, :] + t2[0:1, :])
                ci.append(i1[8:16, :] * PEER_NKEYS + i2[0:1, :])
                cp.append((8.0 + rows(8)) * k)
            cand_v.append(jnp.concatenate(cv, axis=0))
            cand_i.append(jnp.concatenate(ci, axis=0))
            cand_p.append(jnp.concatenate(cp, axis=0))
        bv, _, bi = _extract_top(cand_v, cand_p, k, payload=cand_i)
        best = jnp.concatenate(bv, axis=0)
        e = jnp.exp(best - best[0:1, :])
        g_rows.append(e / jnp.sum(e, axis=0, keepdims=True))
        id_rows.append(jnp.concatenate(bi, axis=0))
    return jnp.concatenate(id_rows, axis=0), jnp.concatenate(g_rows, axis=0)


def _peer_topk_kernel(q_ref, keys_ref, ids_ref, g_ref):
    ids_t, g_t = _topk_heads(q_ref, keys_ref)
    ids_ref[...] = ids_t.astype(jnp.int32)
    g_ref[...] = g_t


def peer_topk(q, sub_keys, tm=256):
    m = q.shape[0]
    return pl.pallas_call(
        _peer_topk_kernel,
        grid=(m // tm,),
        in_specs=[pl.BlockSpec((tm, q.shape[1]), lambda i: (i, 0)),
                  pl.BlockSpec(sub_keys.shape, lambda i: (0, 0, 0, 0))],
        out_specs=[pl.BlockSpec((PEER_SLOTS, tm), lambda i: (0, i))] * 2,
        out_shape=[jax.ShapeDtypeStruct((PEER_SLOTS, m), jnp.int32),
                   jax.ShapeDtypeStruct((PEER_SLOTS, m), F32)],
        compiler_params=_params("parallel"),
        name="peer_topk",
    )(q, sub_keys)


def _transpose_sublane_blocks(x):
    x = list(x)
    sub = lax.broadcasted_iota(jnp.int32, x[0].shape, 0)
    s = SUBLANES // 2
    while s >= 1:
        clear = (sub & s) == 0
        for i in range(SUBLANES):
            if i & s:
                continue
            a, b = x[i], x[i + s]
            x[i] = jnp.where(clear, a, pltpu.roll(b, s, axis=0))
            x[i + s] = jnp.where(clear, pltpu.roll(a, SUBLANES - s, axis=0), b)
        s //= 2
    return x


PEER_GROUP = 16


def _peer_onehot_kernel(ids_t_ref, g_t_ref, w_ref, ids_ref, g_ref, ra_ref, rb_ref):
    tm = ids_ref.shape[0]
    n_grp = tm // PEER_GROUP
    for c in range(tm // LANES):
        cs = slice(c * LANES, (c + 1) * LANES)
        ids_ref[cs, :] = ids_t_ref[:, cs].astype(F32).T.astype(jnp.int32)
        g_ref[cs, :] = g_t_ref[:, cs].T
    sub = lax.broadcasted_iota(jnp.int32, (PEER_NKEYS, PEER_SLOTS), 0)
    zeros = jnp.zeros((PEER_NKEYS, PEER_SLOTS), F32)

    def maps(grp, r_ref):
        m0 = pl.multiple_of(grp * PEER_GROUP, PEER_GROUP)
        for t in range(0, PEER_GROUP, 2):
            gates, hots = [], []
            for tok in (t, t + 1):
                ids = ids_ref[pl.ds(m0 + tok, 1), :]
                gates.append(jnp.where(sub == (ids >> 7), g_ref[pl.ds(m0 + tok, 1), :], 0.0))
                hots.append(jnp.where(sub == (ids & (PEER_NKEYS - 1)), 1.0, 0.0))
            lhs = jnp.concatenate([jnp.concatenate([gates[0], zeros], axis=1),
                                   jnp.concatenate([zeros, gates[1]], axis=1)], axis=0)
            bt = jnp.concatenate(hots, axis=1)
            r = lax.dot_general(lhs.astype(BF16), bt.astype(BF16), _NT, preferred_element_type=F32)
            r_ref[t] = r[:PEER_NKEYS]
            r_ref[t + 1] = r[PEER_NKEYS:]

    def emit(grp, r_ref):
        m0 = pl.multiple_of(grp * PEER_GROUP, PEER_GROUP)
        for gi in range(PEER_NKEYS // SUBLANES):
            rows = slice(gi * SUBLANES, (gi + 1) * SUBLANES)
            ya = _transpose_sublane_blocks([r_ref[t, rows, :] for t in range(SUBLANES)])
            yb = _transpose_sublane_blocks([r_ref[SUBLANES + t, rows, :] for t in range(SUBLANES)])
            for r in range(SUBLANES):
                tile = jnp.concatenate([ya[r], yb[r]], axis=0).astype(BF16)
                w_ref[gi * SUBLANES + r, pl.ds(m0, PEER_GROUP), :] = tile

    maps(0, ra_ref)

    def body(k, carry):
        g0 = 2 * k
        maps(g0 + 1, rb_ref)
        emit(g0, ra_ref)
        maps(g0 + 2, ra_ref)
        emit(g0 + 1, rb_ref)
        return carry

    lax.fori_loop(0, n_grp // 2 - 1, body, 0)
    maps(n_grp - 1, rb_ref)
    emit(n_grp - 2, ra_ref)
    emit(n_grp - 1, rb_ref)


def peer_onehot(ids_t, g_t, tm=256):
    m = ids_t.shape[1]
    assert tm % (2 * PEER_GROUP) == 0 and tm % LANES == 0
    res = pltpu.VMEM((PEER_GROUP, PEER_NKEYS, PEER_NKEYS), F32)
    return pl.pallas_call(
        _peer_onehot_kernel,
        grid=(m // tm,),
        in_specs=[pl.BlockSpec((PEER_SLOTS, tm), lambda i: (0, i))] * 2,
        out_specs=pl.BlockSpec((PEER_NKEYS, tm, PEER_NKEYS), lambda i: (0, i, 0)),
        out_shape=jax.ShapeDtypeStruct((PEER_NKEYS, m, PEER_NKEYS), BF16),
        scratch_shapes=[pltpu.VMEM((tm, PEER_SLOTS), jnp.int32), pltpu.VMEM((tm, PEER_SLOTS), F32), res, res],
        compiler_params=_params("parallel"),
        name="peer_onehot",
    )(ids_t, g_t)


def _peer_dense_kernel(h_ref, gain_ref, w_ref, u_ref, v_ref, fg_ref, o_ref, hn_ref, acc_ref, *,
                       groups, final_norm):
    j = pl.program_id(1)

    @pl.when(j == 0)
    def _():
        hn_ref[...] = _rms(h_ref[...], gain_ref[...]).astype(BF16)
        acc_ref[...] = jnp.zeros_like(acc_ref)

    z = jnp.dot(hn_ref[...], u_ref[...], preferred_element_type=F32)
    wsel = jnp.concatenate([w_ref[gi] for gi in range(groups)], axis=-1)
    act = _gelu_tanh(z).astype(BF16) * wsel
    acc_ref[...] += jnp.dot(act, v_ref[...], preferred_element_type=F32)

    @pl.when(j == pl.num_programs(1) - 1)
    def _():
        out = h_ref[...] + acc_ref[...]
        if final_norm:
            out = _rms(out, fg_ref[...])
        o_ref[...] = out


def peer_dense(h, gain, wmap, ut_bf16, v_bf16, final_gain, final_norm, tm=512, groups=16):
    m, d = h.shape
    ne = v_bf16.shape[0]
    te = groups * PEER_NKEYS
    return pl.pallas_call(
        functools.partial(_peer_dense_kernel, groups=groups, final_norm=final_norm),
        grid=(m // tm, ne // te),
        in_specs=[pl.BlockSpec((tm, d), lambda i, j: (i, 0)),
                  pl.BlockSpec((1, d), lambda i, j: (0, 0)),
                  pl.BlockSpec((groups, tm, PEER_NKEYS), lambda i, j: (j, i, 0)),
                  pl.BlockSpec((d, te), lambda i, j: (0, j)),
                  pl.BlockSpec((te, d), lambda i, j: (j, 0)),
                  pl.BlockSpec((1, d), lambda i, j: (0, 0))],
        out_specs=pl.BlockSpec((tm, d), lambda i, j: (i, 0)),
        out_shape=jax.ShapeDtypeStruct((m, d), F32),
        scratch_shapes=[pltpu.VMEM((tm, d), BF16), pltpu.VMEM((tm, d), F32)],
        compiler_params=_params("parallel", "arbitrary"),
        name="peer_dense",
    )(h, gain.reshape(1, d), wmap, ut_bf16, v_bf16, final_gain.reshape(1, d))


def peer_ffn(h, gain, w_query, sub_keys, u_tab, v_tab, final_gain, final_norm):
    (q,) = norm_matmul(h, gain, w_query.astype(BF16), [w_query.shape[1]], [F32])
    ids, g = peer_topk(q, sub_keys)
    wmap = peer_onehot(ids, g)
    return peer_dense(h, gain, wmap, u_tab.astype(BF16).T, v_tab.astype(BF16), final_gain, final_norm)


def gla_dilated_layer(h, bsz, gain, w_in, w_alpha, b_alpha, head_gain, w_out):
    m, d = h.shape
    t = m // bsz
    sizes = [GLA_QK_W, GLA_QK_W, GLA_V_W, GLA_V_W, GLA_GATE_RANK, ATT_W, ATT_W, ATT_W]
    offs = [0]
    for s in sizes:
        offs.append(offs[-1] + s)
    seg = lambda i: w_in[:, offs[i]:offs[i + 1]]
    pad = jnp.zeros((d, LANES - GLA_GATE_RANK), w_in.dtype)
    w_cat = jnp.concatenate([seg(0), seg(1), seg(2), seg(3), seg(5), seg(6), seg(7), seg(4), pad], axis=1)
    widths = [GLA_QK_W, GLA_QK_W, GLA_V_W, GLA_V_W, ATT_W, ATT_W, ATT_W, LANES]
    dts = [F32] * len(widths)
    gq, gk, gv, gg, aq, ak, av, glr = norm_matmul(h, gain, w_cat.astype(BF16), widths, dts,
                                                  lane_major=(4, 5, 6))
    r3 = lambda a: a.reshape(bsz, t, a.shape[1])
    o_gla = gla(r3(gq), r3(gk), r3(gv), r3(gg), r3(glr), w_alpha, b_alpha, head_gain)
    outs, lses = [], []
    for window, dil in DILATED_BRANCHES:
        o, l = dilated_branch(aq, ak, av, bsz, window, dil)
        outs.append(o)
        lses.append(l)
    return combine_out(o_gla.reshape(m, GLA_V_W), outs, lses, w_out.astype(BF16), h)


def rglru_layer(h, bsz, gain, w_in, conv_w, conv_b, w_r, b_r, w_i, b_i, lam, w_out):
    m, d = h.shape
    t = m // bsz
    d_rnn = w_in.shape[1] // 2
    gate, xr = norm_matmul(h, gain, w_in.astype(BF16), [d_rnn, d_rnn], [F32, F32])
    y = lru(gate.reshape(bsz, t, d_rnn), xr.reshape(bsz, t, d_rnn), conv_w, conv_b, w_r, b_r, w_i, b_i, lam)
    return matmul_res(y.reshape(m, d_rnn), w_out.astype(BF16), h)


def kernel(x, mix_norm, ffn_norm, ab_w_in, gla_w_alpha, gla_b_alpha, gla_head_gain, ab_w_out,
           c_w_in, c_conv_w, c_conv_b, c_w_rgate, c_b_rgate, c_w_igate, c_b_igate, c_lambda, c_w_out,
           peer_w_query, peer_sub_keys, peer_u, peer_v, final_norm):
    bsz, t, d = x.shape
    depth = mix_norm.shape[0]
    h = x.reshape(bsz * t, d)
    for layer in range(depth):
        if layer % 2 == 0:
            e = layer // 2
            h = gla_dilated_layer(h, bsz, mix_norm[layer], ab_w_in[e], gla_w_alpha[e], gla_b_alpha[e],
                                  gla_head_gain[e], ab_w_out[e])
        else:
            o = layer // 2
            h = rglru_layer(h, bsz, mix_norm[layer], c_w_in[o], c_conv_w[o], c_conv_b[o], c_w_rgate[o],
                            c_b_rgate[o], c_w_igate[o], c_b_igate[o], c_lambda[o], c_w_out[o])
        h = peer_ffn(h, ffn_norm[layer], peer_w_query[layer], peer_sub_keys[layer], peer_u[layer],
                     peer_v[layer], final_norm, final_norm=(layer == depth - 1))
    return h.reshape(bsz, t, d)
```

```python
import functools
import math

import jax
import jax.numpy as jnp
from jax import lax
from jax.experimental import pallas as pl
from jax.experimental.pallas import tpu as pltpu

F32 = jnp.float32
BF16 = jnp.bfloat16

NORM_EPS = 1e-6
V7X_VMEM_LIMIT = 56 * 1024 * 1024

GLA_HEADS = 4
GLA_DK = 64
GLA_DV = 128
GLA_GATE_RANK = 16
GLA_GATE_TAU = 16.0
GLA_CHUNK = 64
ATT_HEADS = 8
ATT_HD = 64
DILATED_BRANCHES = ((128, 1), (512, 4), (2048, 16))
GLA_QK_W = GLA_HEADS * GLA_DK
GLA_V_W = GLA_HEADS * GLA_DV
ATT_W = ATT_HEADS * ATT_HD
CONV_W = 4
LRU_C = 8.0
RNN_BLOCK_W = 128
PEER_HEADS = 8
PEER_NKEYS = 128
PEER_TOPK = 16
PEER_SLOTS = PEER_HEADS * PEER_TOPK
LANES = 128
SUBLANES = 8

_NT = (((1,), (1,)), ((), ()))
_TN = (((0,), (0,)), ((), ()))


def _params(*sem):
    return pltpu.CompilerParams(dimension_semantics=sem, vmem_limit_bytes=V7X_VMEM_LIMIT)


def _gelu_tanh(x):
    c = math.sqrt(2.0 / math.pi)
    return (0.5 * x) * (1.0 + jnp.tanh(x * (c + (0.044715 * c) * (x * x))))


def _rms(x, gain):
    ms = jnp.mean(x * x, axis=-1, keepdims=True)
    return x * lax.rsqrt(ms + NORM_EPS) * gain


def _norm_matmul_kernel(x_ref, g_ref, w_ref, *o_refs, widths):
    xn = _rms(x_ref[...], g_ref[...]).astype(BF16)
    off = 0
    for o_ref, w in zip(o_refs, widths):
        y = jnp.dot(xn, w_ref[:, off:off + w], preferred_element_type=F32).astype(o_ref.dtype)
        if len(o_ref.shape) == 3:
            for c in range(w // LANES):
                o_ref[c] = y[:, c * LANES:(c + 1) * LANES]
        else:
            o_ref[...] = y
        off += w


def norm_matmul(x, gain, w_bf16, widths, dtypes, tm=512, lane_major=()):
    m, d = x.shape
    n = w_bf16.shape[1]
    assert sum(widths) == n and m % tm == 0
    out_specs, out_shape = [], []
    for k, (w, dt) in enumerate(zip(widths, dtypes)):
        if k in lane_major:
            out_specs.append(pl.BlockSpec((w // LANES, tm, LANES), lambda i: (0, i, 0)))
            out_shape.append(jax.ShapeDtypeStruct((w // LANES, m, LANES), dt))
        else:
            out_specs.append(pl.BlockSpec((tm, w), lambda i: (i, 0)))
            out_shape.append(jax.ShapeDtypeStruct((m, w), dt))
    return pl.pallas_call(
        functools.partial(_norm_matmul_kernel, widths=tuple(widths)),
        grid=(m // tm,),
        in_specs=[pl.BlockSpec((tm, d), lambda i: (i, 0)),
                  pl.BlockSpec((1, d), lambda i: (0, 0)),
                  pl.BlockSpec((d, n), lambda i: (0, 0))],
        out_specs=out_specs,
        out_shape=out_shape,
        compiler_params=_params("parallel"),
        name="norm_matmul",
    )(x, gain.reshape(1, d), w_bf16)


def _query_epilogue(h_new, g2_ref, wq_ref, q_ref):
    hn = _rms(h_new, g2_ref[...]).astype(BF16)
    q_ref[...] = jnp.dot(hn, wq_ref[...], preferred_element_type=F32)


def _matmul_res_kernel(x_ref, w_ref, r_ref, g2_ref, wq_ref, o_ref, q_ref):
    h_new = r_ref[...] + jnp.dot(x_ref[...].astype(BF16), w_ref[...], preferred_element_type=F32)
    o_ref[...] = h_new
    _query_epilogue(h_new, g2_ref, wq_ref, q_ref)


def matmul_res(x, w_bf16, res, gain2, wq_bf16, tm=512):
    m, k = x.shape
    n = w_bf16.shape[1]
    nq = wq_bf16.shape[1]
    return pl.pallas_call(
        _matmul_res_kernel,
        grid=(m // tm,),
        in_specs=[pl.BlockSpec((tm, k), lambda i: (i, 0)),
                  pl.BlockSpec((k, n), lambda i: (0, 0)),
                  pl.BlockSpec((tm, n), lambda i: (i, 0)),
                  pl.BlockSpec((1, n), lambda i: (0, 0)),
                  pl.BlockSpec((n, nq), lambda i: (0, 0))],
        out_specs=[pl.BlockSpec((tm, n), lambda i: (i, 0)), pl.BlockSpec((tm, nq), lambda i: (i, 0))],
        out_shape=[jax.ShapeDtypeStruct((m, n), F32), jax.ShapeDtypeStruct((m, nq), F32)],
        compiler_params=_params("parallel"),
        name="matmul_res",
    )(x, w_bf16, res, gain2.reshape(1, n), wq_bf16)


def _gla_kernel(q_ref, k_ref, v_ref, gg_ref, lr_ref, wa_ref, ba_ref, hg_ref, o_ref, st_ref, *, tg):
    c = GLA_CHUNK

    @pl.when(pl.program_id(1) == 0)
    def _():
        st_ref[...] = jnp.zeros_like(st_ref)

    z = jnp.dot(lr_ref[0], wa_ref[...], preferred_element_type=F32,
                precision=lax.Precision.HIGHEST) + ba_ref[...]
    log_a = (jnp.minimum(z, 0.0) - jnp.log1p(jnp.exp(-jnp.abs(z)))) * (1.0 / GLA_GATE_TAU)
    nh = GLA_HEADS
    tril = (lax.broadcasted_iota(jnp.int32, (c, c), 0) >= lax.broadcasted_iota(jnp.int32, (c, c), 1)).astype(F32)
    klane_head = lax.broadcasted_iota(jnp.int32, (1, GLA_QK_W), 1) // GLA_DK
    vlane_head = lax.broadcasted_iota(jnp.int32, (1, GLA_V_W), 1) // GLA_DV
    row_head = lax.broadcasted_iota(jnp.int32, (nh * c, 1), 0) // c
    row_t = lax.broadcasted_iota(jnp.int32, (nh * c, c), 0) % c
    causal_rows = row_t >= lax.broadcasted_iota(jnp.int32, (nh * c, c), 1)
    q_rows_mask = row_head == klane_head
    o_rows_mask = row_head == vlane_head
    st_mask = (lax.broadcasted_iota(jnp.int32, (GLA_V_W, 1), 0) // GLA_DV) == klane_head
    for ci in range(tg // c):
        sl = slice(ci * c, (ci + 1) * c)
        b = jnp.dot(tril, log_a[sl], preferred_element_type=F32, precision=lax.Precision.HIGHEST)
        b_last = b[c - 1:c, :]
        q_dec = q_ref[0, sl, :] * (GLA_DK ** -0.5) * jnp.exp(b)
        kk = k_ref[0, sl, :]
        k_inv = (kk * jnp.exp(-b)).astype(BF16)
        k_end = (kk * jnp.exp(b_last - b)).astype(BF16)
        decay = jnp.exp(b_last)
        vv = v_ref[0, sl, :].astype(BF16)
        qb = q_dec.astype(BF16)
        q_stack = jnp.where(q_rows_mask, jnp.concatenate([qb] * nh, axis=0), jnp.zeros((), BF16))
        att = lax.dot_general(q_stack, k_inv, _NT, preferred_element_type=F32)
        att = jnp.where(causal_rows, att, 0.0).astype(BF16)
        o_all = jnp.dot(att, vv, preferred_element_type=F32)
        o_all = jnp.where(o_rows_mask, o_all, 0.0)
        o = o_all[0:c]
        for h in range(1, nh):
            o = o + o_all[h * c:(h + 1) * c]
        st = st_ref[...]
        o = o + lax.dot_general(qb, st.astype(BF16), _NT, preferred_element_type=F32)
        kv_t = lax.dot_general(vv, k_end, _TN, preferred_element_type=F32)
        st_ref[...] = decay * st + jnp.where(st_mask, kv_t, 0.0)
        outs = []
        for h in range(nh):
            o_h = o[:, h * GLA_DV:(h + 1) * GLA_DV]
            outs.append(o_h * lax.rsqrt(jnp.mean(o_h * o_h, axis=-1, keepdims=True) + NORM_EPS))
        o = jnp.concatenate(outs, axis=-1)
        gg = gg_ref[0, sl, :]
        o_ref[0, sl, :] = o * hg_ref[...] * (gg * jax.nn.sigmoid(gg))


def gla(gq, gk, gv, gg, glr, w_alpha, b_alpha, head_gain, tg=256):
    bsz, t, _ = gq.shape
    wa = jnp.zeros((LANES, GLA_QK_W), F32).at[:GLA_GATE_RANK].set(w_alpha)
    spec = lambda w: pl.BlockSpec((1, tg, w), lambda b, i: (b, i, 0))
    const = lambda r, w: pl.BlockSpec((r, w), lambda b, i: (0, 0))
    return pl.pallas_call(
        functools.partial(_gla_kernel, tg=tg),
        grid=(bsz, t // tg),
        in_specs=[spec(GLA_QK_W), spec(GLA_QK_W), spec(GLA_V_W), spec(GLA_V_W), spec(LANES),
                  const(LANES, GLA_QK_W), const(1, GLA_QK_W), const(1, GLA_V_W)],
        out_specs=spec(GLA_V_W),
        out_shape=jax.ShapeDtypeStruct((bsz, t, GLA_V_W), F32),
        scratch_shapes=[pltpu.VMEM((GLA_V_W, GLA_QK_W), F32)],
        compiler_params=_params("parallel", "arbitrary"),
        name="gla",
    )(gq, gk, gv, gg, glr, wa, b_alpha.reshape(1, -1), head_gain.reshape(1, -1))


def _dilated_kernel(q_ref, kp_ref, kc_ref, vp_ref, vc_ref, o_ref, l_ref, *, dil, blk):
    i = pl.program_id(1)
    cblk = pl.program_id(2)
    n_pairs = q_ref.shape[0]
    n_cblk = ATT_HEADS // 2 // n_pairs
    qi = lax.broadcasted_iota(jnp.int32, (blk, 2 * blk), 0)
    kj = lax.broadcasted_iota(jnp.int32, (blk, 2 * blk), 1)
    delta = qi + blk - kj
    kj_min = jnp.where(i > 0, 0, blk)
    valid = (delta >= 0) & (delta <= blk) & (kj >= kj_min)
    dist = (delta * dil).astype(F32)
    lane = lax.broadcasted_iota(jnp.int32, (1, 2 * ATT_HD), 1)
    lo = lane < ATT_HD

    def alibi(lp, e):
        slopes = [2.0 ** (-8.0 * (2 * (c * n_pairs + lp) + e + 1) / ATT_HEADS) for c in range(n_cblk)]
        slope = jnp.float32(slopes[-1])
        for c in range(n_cblk - 2, -1, -1):
            slope = jnp.where(cblk == c, jnp.float32(slopes[c]), slope)
        return jnp.where(valid, -slope * dist, -1e30)

    biases = [[alibi(lp, e) for e in range(2)] for lp in range(n_pairs)]

    per_trip = max(1, min(dil, ATT_HEADS // (2 * n_pairs)))

    def body(trip, carry):
        for sub in range(per_trip):
            one_class(trip + sub * (dil // per_trip))
        return carry

    def one_class(r):
        rows = pl.ds(r, blk, stride=dil)
        for lp in range(n_pairs):
            q2 = (q_ref[lp, rows, :] * (ATT_HD ** -0.5)).astype(BF16)
            k2 = jnp.concatenate([kp_ref[lp, rows, :], kc_ref[lp, rows, :]], axis=0).astype(BF16)
            v2 = jnp.concatenate([vp_ref[lp, rows, :], vc_ref[lp, rows, :]], axis=0).astype(BF16)
            o_pair = jnp.zeros((blk, 2 * ATT_HD), F32)
            lse_pair = jnp.zeros((blk, 2 * ATT_HD), F32)
            for e in range(2):
                sel = lo if e == 0 else jnp.logical_not(lo)
                qm = jnp.where(sel, q2, jnp.zeros_like(q2))
                vm = jnp.where(sel, v2, jnp.zeros_like(v2))
                s = lax.dot_general(qm, k2, _NT, preferred_element_type=F32) + biases[lp][e]
                m = jnp.max(s, axis=-1, keepdims=True)
                pexp = jnp.exp(s - m)
                l = jnp.sum(pexp, axis=-1, keepdims=True)
                o_e = jnp.dot(pexp.astype(BF16), vm, preferred_element_type=F32)
                o_pair = o_pair + o_e * (1.0 / l)
                lse_pair = jnp.where(sel, m + jnp.log(l), lse_pair)
            o_ref[lp, rows, :] = o_pair
            l_ref[lp, rows, :] = lse_pair

    if dil == per_trip:
        body(0, 0)
    else:
        lax.fori_loop(0, dil // per_trip, body, 0)


def dilated_branch(q, k, v, bsz, window, dil):
    n_slab, m, pw = q.shape
    t = m // bsz
    blk = window // dil
    rows = blk * dil
    assert t % rows == 0 and pw == 2 * ATT_HD
    nblk = t // rows
    n_pairs = n_slab if rows <= 512 else n_slab // 2
    cur = pl.BlockSpec((n_pairs, rows, pw), lambda b, i, c: (c, b * nblk + i, 0))
    prev = pl.BlockSpec((n_pairs, rows, pw), lambda b, i, c: (c, b * nblk + jnp.maximum(i - 1, 0), 0))
    return pl.pallas_call(
        functools.partial(_dilated_kernel, dil=dil, blk=blk),
        grid=(bsz, nblk, n_slab // n_pairs),
        in_specs=[cur, prev, cur, prev, cur],
        out_specs=[cur, cur],
        out_shape=[jax.ShapeDtypeStruct((n_slab, m, pw), F32)] * 2,
        compiler_params=_params("parallel", "arbitrary", "arbitrary"),
        name=f"dilated_d{dil}",
    )(q, k, k, v, v)


def _combine_out_kernel(og_ref, o1_ref, o2_ref, o3_ref, l1_ref, l2_ref, l3_ref, w_ref, r_ref, g2_ref, wq_ref,
                        out_ref, q_ref):
    wide = lambda ref: jnp.concatenate([ref[c] for c in range(ref.shape[0])], axis=-1)
    l1, l2, l3 = wide(l1_ref), wide(l2_ref), wide(l3_ref)
    mx = jnp.maximum(jnp.maximum(l1, l2), l3)
    e1, e2, e3 = jnp.exp(l1 - mx), jnp.exp(l2 - mx), jnp.exp(l3 - mx)
    att = (e1 * wide(o1_ref) + e2 * wide(o2_ref) + e3 * wide(o3_ref)) / (e1 + e2 + e3)
    kg = og_ref.shape[1]
    acc = jnp.dot(og_ref[...].astype(BF16), w_ref[:kg, :], preferred_element_type=F32)
    acc = acc + jnp.dot(att.astype(BF16), w_ref[kg:, :], preferred_element_type=F32)
    h_new = r_ref[...] + acc
    out_ref[...] = h_new
    _query_epilogue(h_new, g2_ref, wq_ref, q_ref)


def combine_out(o_gla, outs, lses, w_bf16, res, gain2, wq_bf16, tm=512):
    m = res.shape[0]
    n = w_bf16.shape[1]
    nq = wq_bf16.shape[1]
    row = lambda w: pl.BlockSpec((tm, w), lambda i: (i, 0))
    slab = pl.BlockSpec((ATT_W // LANES, tm, LANES), lambda i: (0, i, 0))
    return pl.pallas_call(
        _combine_out_kernel,
        grid=(m // tm,),
        in_specs=[row(GLA_V_W)] + [slab] * 6
                 + [pl.BlockSpec(w_bf16.shape, lambda i: (0, 0)), row(n),
                    pl.BlockSpec((1, n), lambda i: (0, 0)), pl.BlockSpec((n, nq), lambda i: (0, 0))],
        out_specs=[row(n), row(nq)],
        out_shape=[jax.ShapeDtypeStruct((m, n), F32), jax.ShapeDtypeStruct((m, nq), F32)],
        compiler_params=_params("parallel"),
        name="combine_out",
    )(o_gla, *outs, *lses, w_bf16, res, gain2.reshape(1, n), wq_bf16)


def _lru_kernel(gate_ref, xr_ref, cw_ref, cb_ref, wr_ref, br_ref, wi_ref, bi_ref, lam_ref, y_ref,
                ext_ref, h_ref, *, tt):
    @pl.when(pl.program_id(1) == 0)
    def _():
        ext_ref[:SUBLANES, :] = jnp.zeros((SUBLANES, ext_ref.shape[1]), F32)
        h_ref[...] = jnp.zeros_like(h_ref)

    xr = xr_ref[0]
    d = xr.shape[1]
    ext_ref[SUBLANES:, :] = xr
    xc = cb_ref[...] + cw_ref[CONV_W - 1:CONV_W, :] * xr
    for j in range(CONV_W - 1):
        off = SUBLANES - (CONV_W - 1) + j
        xc = xc + cw_ref[j:j + 1, :] * ext_ref[off:off + tt, :]
    ext_ref[:SUBLANES, :] = xr[tt - SUBLANES:, :]
    xcb = xc.astype(BF16)
    rs, is_ = [], []
    for g in range(d // RNN_BLOCK_W):
        cs = slice(g * RNN_BLOCK_W, (g + 1) * RNN_BLOCK_W)
        rs.append(jnp.dot(xcb[:, cs], wr_ref[g], preferred_element_type=F32))
        is_.append(jnp.dot(xcb[:, cs], wi_ref[g], preferred_element_type=F32))
    r = jax.nn.sigmoid(jnp.concatenate(rs, axis=-1) + br_ref[...])
    ig = jax.nn.sigmoid(jnp.concatenate(is_, axis=-1) + bi_ref[...])
    lam = lam_ref[...]
    softplus = jnp.maximum(-lam, 0.0) + jnp.log1p(jnp.exp(-jnp.abs(lam)))
    log_a = (-LRU_C) * r * softplus
    a = jnp.exp(log_a)
    th = jnp.tanh(log_a)
    n = -2.0 * th
    u = jnp.where(n > 0.0, n * lax.rsqrt(n), 0.0) * lax.rsqrt(1.0 - th) * (ig * xc)
    a = a.reshape(tt // SUBLANES, SUBLANES, d)
    u = u.reshape(tt // SUBLANES, SUBLANES, d)
    rows = lax.broadcasted_iota(jnp.int32, (1, SUBLANES, 1), 1)
    s = 1
    while s < SUBLANES:
        keep = rows >= s
        a_sh = jnp.where(keep, pltpu.roll(a, s, axis=1), 1.0)
        u_sh = jnp.where(keep, pltpu.roll(u, s, axis=1), 0.0)
        u = u + a * u_sh
        a = a * a_sh
        s *= 2
    carry = h_ref[...]
    gate = gate_ref[0]
    for g in range(tt // SUBLANES):
        hs = u[g] + a[g] * carry
        carry = hs[SUBLANES - 1:SUBLANES, :]
        y_ref[0, g * SUBLANES:(g + 1) * SUBLANES, :] = _gelu_tanh(gate[g * SUBLANES:(g + 1) * SUBLANES]) * hs
    h_ref[...] = carry


def lru(gate, xr, conv_w, conv_b, w_r, b_r, w_i, b_i, lam, tt=256):
    bsz, t, d = xr.shape
    spec = pl.BlockSpec((1, tt, d), lambda b, i: (b, i, 0))
    row = pl.BlockSpec((1, d), lambda b, i: (0, 0))
    wspec = pl.BlockSpec(w_r.shape, lambda b, i: (0, 0, 0))
    return pl.pallas_call(
        functools.partial(_lru_kernel, tt=tt),
        grid=(bsz, t // tt),
        in_specs=[spec, spec, pl.BlockSpec((CONV_W, d), lambda b, i: (0, 0)), row,
                  wspec, row, wspec, row, row],
        out_specs=spec,
        out_shape=jax.ShapeDtypeStruct((bsz, t, d), F32),
        scratch_shapes=[pltpu.VMEM((tt + SUBLANES, d), F32), pltpu.VMEM((1, d), F32)],
        compiler_params=_params("parallel", "arbitrary"),
        name="lru",
    )(gate, xr, conv_w, conv_b.reshape(1, d), w_r.astype(BF16), b_r.reshape(1, d),
      w_i.astype(BF16), b_i.reshape(1, d), lam.reshape(1, d))


def _extract_top(vals, pos, n_rounds, payload=None):
    return _extract_top_many([(vals, pos, payload)], n_rounds)[0]


def _extract_top_many(problems, n_rounds):
    state, outs = [], []
    for vals, pos, payload in problems:
        lo_wins = vals[0] >= vals[1]
        win = lambda x, lo_wins=lo_wins: jnp.where(lo_wins, x[0], x[1])
        lose = lambda x, lo_wins=lo_wins: jnp.where(lo_wins, x[1], x[0])
        st = dict(v_w=win(vals), v_l=lose(vals), p_w=win(pos), p_l=lose(pos))
        if payload is not None:
            st.update(x_w=win(payload), x_l=lose(payload))
        state.append(st)
        outs.append(([], [], []))
    for _ in range(n_rounds):
        for st, (out_v, out_p, out_x) in zip(state, outs):
            m = jnp.max(st["v_w"], axis=0, keepdims=True)
            p = jnp.min(jnp.where(st["v_w"] == m, st["p_w"], 1e9), axis=0, keepdims=True)
            first = st["p_w"] == p
            if "x_w" in st:
                out_x.append(jnp.max(jnp.where(first, st["x_w"], -1.0), axis=0, keepdims=True))
                st["x_w"] = jnp.where(first, st["x_l"], st["x_w"])
            st["v_w"] = jnp.where(first, st["v_l"], st["v_w"])
            st["p_w"] = jnp.where(first, st["p_l"], st["p_w"])
            st["v_l"] = jnp.where(first, -jnp.inf, st["v_l"])
            out_v.append(m)
            out_p.append(p)
    return outs


def _topk_heads(q_ref, keys_ref):
    k = PEER_TOPK
    tm = q_ref.shape[0]
    n_heads = keys_ref.shape[0]
    rows = lambda n: lax.broadcasted_iota(jnp.int32, (n, tm), 0).astype(F32)
    half = PEER_NKEYS // 2
    key_pos = (rows(half), rows(half) + half)
    halves = ([(0, 16), (1, 8), (2, 8), (3, 8)], [(4, 8), (5, 8), (6, 8), (7, 8)])
    id_rows, g_rows = [], []
    for h in range(n_heads):
        scores = []
        for p in range(2):
            keys = keys_ref[h, p]
            qs = q_ref[:, (2 * h + p) * LANES:(2 * h + p + 1) * LANES]
            s_t = tuple(lax.dot_general(keys[kh * half:(kh + 1) * half], qs, _NT, preferred_element_type=F32,
                                        precision=lax.Precision.HIGHEST) for kh in range(2))
            scores.append((s_t, key_pos, None))
        tops = [(jnp.concatenate(v, axis=0), jnp.concatenate(pidx, axis=0))
                for v, pidx, _ in _extract_top_many(scores, k)]
        (t1, i1), (t2, i2) = tops
        cand_v, cand_i, cand_p = [], [], []
        for hi, groups in enumerate(halves):
            cv = [t1[a:a + 1, :] + t2[0:nb, :] for a, nb in groups]
            ci = [i1[a:a + 1, :] * PEER_NKEYS + i2[0:nb, :] for a, nb in groups]
            cp = [a * k + rows(nb) for a, nb in groups]
            if hi == 1:
                cv.append(t1[8:16, :] + t2[0:1, :])
                ci.append(i1[8:16, :] * PEER_NKEYS + i2[0:1, :])
                cp.append((8.0 + rows(8)) * k)
            cand_v.append(jnp.concatenate(cv, axis=0))
            cand_i.append(jnp.concatenate(ci, axis=0))
            cand_p.append(jnp.concatenate(cp, axis=0))
        bv, _, bi = _extract_top(cand_v, cand_p, k, payload=cand_i)
        best = jnp.concatenate(bv, axis=0)
        e = jnp.exp(best - best[0:1, :])
        g_rows.append(e / jnp.sum(e, axis=0, keepdims=True))
        id_rows.append(jnp.concatenate(bi, axis=0))
    return jnp.concatenate(id_rows, axis=0), jnp.concatenate(g_rows, axis=0)


def _peer_topk_kernel(q_ref, keys_ref, ids_ref, g_ref):
    ids_t, g_t = _topk_heads(q_ref, keys_ref)
    ids_ref[...] = ids_t.astype(jnp.int32)
    g_ref[...] = g_t


def peer_topk(q, sub_keys, tm=256):
    m = q.shape[0]
    return pl.pallas_call(
        _peer_topk_kernel,
        grid=(m // tm,),
        in_specs=[pl.BlockSpec((tm, q.shape[1]), lambda i: (i, 0)),
                  pl.BlockSpec(sub_keys.shape, lambda i: (0, 0, 0, 0))],
        out_specs=[pl.BlockSpec((PEER_SLOTS, tm), lambda i: (0, i))] * 2,
        out_shape=[jax.ShapeDtypeStruct((PEER_SLOTS, m), jnp.int32),
                   jax.ShapeDtypeStruct((PEER_SLOTS, m), F32)],
        compiler_params=_params("parallel"),
        name="peer_topk",
    )(q, sub_keys)


def _transpose_sublane_blocks(x):
    x = list(x)
    sub = lax.broadcasted_iota(jnp.int32, x[0].shape, 0)
    s = SUBLANES // 2
    while s >= 1:
        clear = (sub & s) == 0
        for i in range(SUBLANES):
            if i & s:
                continue
            a, b = x[i], x[i + s]
            x[i] = jnp.where(clear, a, pltpu.roll(b, s, axis=0))
            x[i + s] = jnp.where(clear, pltpu.roll(a, SUBLANES - s, axis=0), b)
        s //= 2
    return x


PEER_GROUP = 16


def _peer_onehot_kernel(ids_t_ref, g_t_ref, w_ref, ids_ref, g_ref, ra_ref, rb_ref):
    tm = ids_ref.shape[0]
    n_grp = tm // PEER_GROUP
    for c in range(tm // LANES):
        cs = slice(c * LANES, (c + 1) * LANES)
        ids_ref[cs, :] = ids_t_ref[:, cs].astype(F32).T.astype(jnp.int32)
        g_ref[cs, :] = g_t_ref[:, cs].T
    sub = lax.broadcasted_iota(jnp.int32, (PEER_NKEYS, PEER_SLOTS), 0)
    zeros = jnp.zeros((PEER_NKEYS, PEER_SLOTS), F32)

    def maps(grp, r_ref):
        m0 = pl.multiple_of(grp * PEER_GROUP, PEER_GROUP)
        for t in range(0, PEER_GROUP, 2):
            gates, hots = [], []
            for tok in (t, t + 1):
                ids = ids_ref[pl.ds(m0 + tok, 1), :]
                gates.append(jnp.where(sub == (ids >> 7), g_ref[pl.ds(m0 + tok, 1), :], 0.0))
                hots.append(jnp.where(sub == (ids & (PEER_NKEYS - 1)), 1.0, 0.0))
            lhs = jnp.concatenate([jnp.concatenate([gates[0], zeros], axis=1),
                                   jnp.concatenate([zeros, gates[1]], axis=1)], axis=0)
            bt = jnp.concatenate(hots, axis=1)
            r = lax.dot_general(lhs.astype(BF16), bt.astype(BF16), _NT, preferred_element_type=F32)
            r_ref[t] = r[:PEER_NKEYS]
            r_ref[t + 1] = r[PEER_NKEYS:]

    def emit(grp, r_ref):
        m0 = pl.multiple_of(grp * PEER_GROUP, PEER_GROUP)
        for gi in range(PEER_NKEYS // SUBLANES):
            rows = slice(gi * SUBLANES, (gi + 1) * SUBLANES)
            ya = _transpose_sublane_blocks([r_ref[t, rows, :] for t in range(SUBLANES)])
            yb = _transpose_sublane_blocks([r_ref[SUBLANES + t, rows, :] for t in range(SUBLANES)])
            for r in range(SUBLANES):
                tile = jnp.concatenate([ya[r], yb[r]], axis=0).astype(BF16)
                w_ref[gi * SUBLANES + r, pl.ds(m0, PEER_GROUP), :] = tile

    maps(0, ra_ref)

    def body(k, carry):
        g0 = 2 * k
        maps(g0 + 1, rb_ref)
        emit(g0, ra_ref)
        maps(g0 + 2, ra_ref)
        emit(g0 + 1, rb_ref)
        return carry

    lax.fori_loop(0, n_grp // 2 - 1, body, 0)
    maps(n_grp - 1, rb_ref)
    emit(n_grp - 2, ra_ref)
    emit(n_grp - 1, rb_ref)


def peer_onehot(ids_t, g_t, tm=256):
    m = ids_t.shape[1]
    assert tm % (2 * PEER_GROUP) == 0 and tm % LANES == 0
    res = pltpu.VMEM((PEER_GROUP, PEER_NKEYS, PEER_NKEYS), F32)
    return pl.pallas_call(
        _peer_onehot_kernel,
        grid=(m // tm,),
        in_specs=[pl.BlockSpec((PEER_SLOTS, tm), lambda i: (0, i))] * 2,
        out_specs=pl.BlockSpec((PEER_NKEYS, tm, PEER_NKEYS), lambda i: (0, i, 0)),
        out_shape=jax.ShapeDtypeStruct((PEER_NKEYS, m, PEER_NKEYS), BF16),
        scratch_shapes=[pltpu.VMEM((tm, PEER_SLOTS), jnp.int32), pltpu.VMEM((tm, PEER_SLOTS), F32), res, res],
        compiler_params=_params("parallel"),
        name="peer_onehot",
    )(ids_t, g_t)


def _peer_dense_kernel(h_ref, gain_ref, w_ref, u_ref, v_ref, fg_ref, o_ref, hn_ref, acc_ref, *,
                       groups, final_norm):
    j = pl.program_id(1)

    @pl.when(j == 0)
    def _():
        hn_ref[...] = _rms(h_ref[...], gain_ref[...]).astype(BF16)
        acc_ref[...] = jnp.zeros_like(acc_ref)

    z = lax.dot_general(hn_ref[...], u_ref[...], _NT, preferred_element_type=F32)
    wsel = jnp.concatenate([w_ref[gi] for gi in range(groups)], axis=-1)
    act = _gelu_tanh(z).astype(BF16) * wsel
    acc_ref[...] += jnp.dot(act, v_ref[...], preferred_element_type=F32)

    @pl.when(j == pl.num_programs(1) - 1)
    def _():
        out = h_ref[...] + acc_ref[...]
        if final_norm:
            out = _rms(out, fg_ref[...])
        o_ref[...] = out


def peer_dense(h, gain, wmap, u_bf16, v_bf16, final_gain, final_norm, tm=512, groups=16):
    m, d = h.shape
    ne = v_bf16.shape[0]
    te = groups * PEER_NKEYS
    return pl.pallas_call(
        functools.partial(_peer_dense_kernel, groups=groups, final_norm=final_norm),
        grid=(m // tm, ne // te),
        in_specs=[pl.BlockSpec((tm, d), lambda i, j: (i, 0)),
                  pl.BlockSpec((1, d), lambda i, j: (0, 0)),
                  pl.BlockSpec((groups, tm, PEER_NKEYS), lambda i, j: (j, i, 0)),
                  pl.BlockSpec((te, d), lambda i, j: (j, 0)),
                  pl.BlockSpec((te, d), lambda i, j: (j, 0)),
                  pl.BlockSpec((1, d), lambda i, j: (0, 0))],
        out_specs=pl.BlockSpec((tm, d), lambda i, j: (i, 0)),
        out_shape=jax.ShapeDtypeStruct((m, d), F32),
        scratch_shapes=[pltpu.VMEM((tm, d), BF16), pltpu.VMEM((tm, d), F32)],
        compiler_params=_params("parallel", "arbitrary"),
        name="peer_dense",
    )(h, gain.reshape(1, d), wmap, u_bf16, v_bf16, final_gain.reshape(1, d))


def peer_ffn(h, q, gain, sub_keys, u_tab, v_tab, final_gain, final_norm):
    ids, g = peer_topk(q, sub_keys)
    wmap = peer_onehot(ids, g)
    return peer_dense(h, gain, wmap, u_tab.astype(BF16), v_tab.astype(BF16), final_gain, final_norm)


def gla_dilated_layer(h, bsz, gain, w_in, w_alpha, b_alpha, head_gain, w_out, ffn_gain, w_query):
    m, d = h.shape
    t = m // bsz
    sizes = [GLA_QK_W, GLA_QK_W, GLA_V_W, GLA_V_W, GLA_GATE_RANK, ATT_W, ATT_W, ATT_W]
    offs = [0]
    for s in sizes:
        offs.append(offs[-1] + s)
    seg = lambda i: w_in[:, offs[i]:offs[i + 1]]
    pad = jnp.zeros((d, LANES - GLA_GATE_RANK), w_in.dtype)
    w_cat = jnp.concatenate([seg(0), seg(1), seg(2), seg(3), seg(5), seg(6), seg(7), seg(4), pad], axis=1)
    widths = [GLA_QK_W, GLA_QK_W, GLA_V_W, GLA_V_W, ATT_W, ATT_W, ATT_W, LANES]
    dts = [F32] * len(widths)
    gq, gk, gv, gg, aq, ak, av, glr = norm_matmul(h, gain, w_cat.astype(BF16), widths, dts,
                                                  lane_major=(4, 5, 6))
    r3 = lambda a: a.reshape(bsz, t, a.shape[1])
    o_gla = gla(r3(gq), r3(gk), r3(gv), r3(gg), r3(glr), w_alpha, b_alpha, head_gain)
    outs, lses = [], []
    for window, dil in DILATED_BRANCHES:
        o, l = dilated_branch(aq, ak, av, bsz, window, dil)
        outs.append(o)
        lses.append(l)
    return combine_out(o_gla.reshape(m, GLA_V_W), outs, lses, w_out.astype(BF16), h, ffn_gain,
                       w_query.astype(BF16))


def rglru_layer(h, bsz, gain, w_in, conv_w, conv_b, w_r, b_r, w_i, b_i, lam, w_out, ffn_gain, w_query):
    m, d = h.shape
    t = m // bsz
    d_rnn = w_in.shape[1] // 2
    gate, xr = norm_matmul(h, gain, w_in.astype(BF16), [d_rnn, d_rnn], [F32, F32])
    y = lru(gate.reshape(bsz, t, d_rnn), xr.reshape(bsz, t, d_rnn), conv_w, conv_b, w_r, b_r, w_i, b_i, lam)
    return matmul_res(y.reshape(m, d_rnn), w_out.astype(BF16), h, ffn_gain, w_query.astype(BF16))


def kernel(x, mix_norm, ffn_norm, ab_w_in, gla_w_alpha, gla_b_alpha, gla_head_gain, ab_w_out,
           c_w_in, c_conv_w, c_conv_b, c_w_rgate, c_b_rgate, c_w_igate, c_b_igate, c_lambda, c_w_out,
           peer_w_query, peer_sub_keys, peer_u, peer_v, final_norm):
    bsz, t, d = x.shape
    depth = mix_norm.shape[0]
    h = x.reshape(bsz * t, d)
    for layer in range(depth):
        if layer % 2 == 0:
            e = layer // 2
            h, q = gla_dilated_layer(h, bsz, mix_norm[layer], ab_w_in[e], gla_w_alpha[e], gla_b_alpha[e],
                                     gla_head_gain[e], ab_w_out[e], ffn_norm[layer], peer_w_query[layer])
        else:
            o = layer // 2
            h, q = rglru_layer(h, bsz, mix_norm[layer], c_w_in[o], c_conv_w[o], c_conv_b[o], c_w_rgate[o],
                               c_b_rgate[o], c_w_igate[o], c_b_igate[o], c_lambda[o], c_w_out[o],
                               ffn_norm[layer], peer_w_query[layer])
        h = peer_ffn(h, q, ffn_norm[layer], peer_sub_keys[layer], peer_u[layer], peer_v[layer],
                     final_norm, final_norm=(layer == depth - 1))
    return h.reshape(bsz, t, d)
```

```python
import functools
import math

import jax
import jax.numpy as jnp
from jax import lax
from jax.experimental import pallas as pl
from jax.experimental.pallas import tpu as pltpu

F32 = jnp.float32
BF16 = jnp.bfloat16

NORM_EPS = 1e-6
V7X_VMEM_LIMIT = 56 * 1024 * 1024

GLA_HEADS = 4
GLA_DK = 64
GLA_DV = 128
GLA_GATE_RANK = 16
GLA_GATE_TAU = 16.0
GLA_CHUNK = 64
ATT_HEADS = 8
ATT_HD = 64
DILATED_BRANCHES = ((128, 1), (512, 4), (2048, 16))
GLA_QK_W = GLA_HEADS * GLA_DK
GLA_V_W = GLA_HEADS * GLA_DV
ATT_W = ATT_HEADS * ATT_HD
CONV_W = 4
LRU_C = 8.0
RNN_BLOCK_W = 128
PEER_HEADS = 8
PEER_NKEYS = 128
PEER_TOPK = 16
PEER_SLOTS = PEER_HEADS * PEER_TOPK
LANES = 128
SUBLANES = 8

_NT = (((1,), (1,)), ((), ()))
_TN = (((0,), (0,)), ((), ()))


def _params(*sem):
    return pltpu.CompilerParams(dimension_semantics=sem, vmem_limit_bytes=V7X_VMEM_LIMIT)


def _gelu_tanh(x):
    c = math.sqrt(2.0 / math.pi)
    return (0.5 * x) * (1.0 + jnp.tanh(x * (c + (0.044715 * c) * (x * x))))


def _rms(x, gain):
    ms = jnp.mean(x * x, axis=-1, keepdims=True)
    return x * lax.rsqrt(ms + NORM_EPS) * gain


def _norm_matmul_kernel(x_ref, g_ref, w_ref, *o_refs, widths):
    xn = _rms(x_ref[...], g_ref[...]).astype(BF16)
    off = 0
    for o_ref, w in zip(o_refs, widths):
        y = jnp.dot(xn, w_ref[:, off:off + w], preferred_element_type=F32).astype(o_ref.dtype)
        if len(o_ref.shape) == 3:
            for c in range(w // LANES):
                o_ref[c] = y[:, c * LANES:(c + 1) * LANES]
        else:
            o_ref[...] = y
        off += w


def norm_matmul(x, gain, w_bf16, widths, dtypes, tm=512, lane_major=()):
    m, d = x.shape
    n = w_bf16.shape[1]
    assert sum(widths) == n and m % tm == 0
    out_specs, out_shape = [], []
    for k, (w, dt) in enumerate(zip(widths, dtypes)):
        if k in lane_major:
            out_specs.append(pl.BlockSpec((w // LANES, tm, LANES), lambda i: (0, i, 0)))
            out_shape.append(jax.ShapeDtypeStruct((w // LANES, m, LANES), dt))
        else:
            out_specs.append(pl.BlockSpec((tm, w), lambda i: (i, 0)))
            out_shape.append(jax.ShapeDtypeStruct((m, w), dt))
    return pl.pallas_call(
        functools.partial(_norm_matmul_kernel, widths=tuple(widths)),
        grid=(m // tm,),
        in_specs=[pl.BlockSpec((tm, d), lambda i: (i, 0)),
                  pl.BlockSpec((1, d), lambda i: (0, 0)),
                  pl.BlockSpec((d, n), lambda i: (0, 0))],
        out_specs=out_specs,
        out_shape=out_shape,
        compiler_params=_params("parallel"),
        name="norm_matmul",
    )(x, gain.reshape(1, d), w_bf16)


def _query_epilogue(h_new, g2_ref, wq_ref, q_ref):
    hn = _rms(h_new, g2_ref[...]).astype(BF16)
    q_ref[...] = jnp.dot(hn, wq_ref[...], preferred_element_type=F32)


def _matmul_res_kernel(x_ref, w_ref, r_ref, g2_ref, wq_ref, o_ref, q_ref):
    h_new = r_ref[...] + jnp.dot(x_ref[...].astype(BF16), w_ref[...], preferred_element_type=F32)
    o_ref[...] = h_new
    _query_epilogue(h_new, g2_ref, wq_ref, q_ref)


def matmul_res(x, w_bf16, res, gain2, wq_bf16, tm=512):
    m, k = x.shape
    n = w_bf16.shape[1]
    nq = wq_bf16.shape[1]
    return pl.pallas_call(
        _matmul_res_kernel,
        grid=(m // tm,),
        in_specs=[pl.BlockSpec((tm, k), lambda i: (i, 0)),
                  pl.BlockSpec((k, n), lambda i: (0, 0)),
                  pl.BlockSpec((tm, n), lambda i: (i, 0)),
                  pl.BlockSpec((1, n), lambda i: (0, 0)),
                  pl.BlockSpec((n, nq), lambda i: (0, 0))],
        out_specs=[pl.BlockSpec((tm, n), lambda i: (i, 0)), pl.BlockSpec((tm, nq), lambda i: (i, 0))],
        out_shape=[jax.ShapeDtypeStruct((m, n), F32), jax.ShapeDtypeStruct((m, nq), F32)],
        compiler_params=_params("parallel"),
        name="matmul_res",
    )(x, w_bf16, res, gain2.reshape(1, n), wq_bf16)


def _gla_kernel(q_ref, k_ref, v_ref, gg_ref, lr_ref, wa_ref, ba_ref, hg_ref, o_ref, st_ref, *, tg):
    c = GLA_CHUNK

    @pl.when(pl.program_id(1) == 0)
    def _():
        st_ref[...] = jnp.zeros_like(st_ref)

    z = jnp.dot(lr_ref[0], wa_ref[...], preferred_element_type=F32,
                precision=lax.Precision.HIGHEST) + ba_ref[...]
    log_a = (jnp.minimum(z, 0.0) - jnp.log1p(jnp.exp(-jnp.abs(z)))) * (1.0 / GLA_GATE_TAU)
    nh = GLA_HEADS
    tril = (lax.broadcasted_iota(jnp.int32, (c, c), 0) >= lax.broadcasted_iota(jnp.int32, (c, c), 1)).astype(F32)
    klane_head = lax.broadcasted_iota(jnp.int32, (1, GLA_QK_W), 1) // GLA_DK
    vlane_head = lax.broadcasted_iota(jnp.int32, (1, GLA_V_W), 1) // GLA_DV
    row_head = lax.broadcasted_iota(jnp.int32, (nh * c, 1), 0) // c
    row_t = lax.broadcasted_iota(jnp.int32, (nh * c, c), 0) % c
    causal_rows = row_t >= lax.broadcasted_iota(jnp.int32, (nh * c, c), 1)
    q_rows_mask = row_head == klane_head
    o_rows_mask = row_head == vlane_head
    st_mask = (lax.broadcasted_iota(jnp.int32, (GLA_V_W, 1), 0) // GLA_DV) == klane_head
    for ci in range(tg // c):
        sl = slice(ci * c, (ci + 1) * c)
        b = jnp.dot(tril, log_a[sl], preferred_element_type=F32, precision=lax.Precision.HIGHEST)
        b_last = b[c - 1:c, :]
        q_dec = q_ref[0, sl, :] * (GLA_DK ** -0.5) * jnp.exp(b)
        kk = k_ref[0, sl, :]
        k_inv = (kk * jnp.exp(-b)).astype(BF16)
        k_end = (kk * jnp.exp(b_last - b)).astype(BF16)
        decay = jnp.exp(b_last)
        vv = v_ref[0, sl, :].astype(BF16)
        qb = q_dec.astype(BF16)
        q_stack = jnp.where(q_rows_mask, jnp.concatenate([qb] * nh, axis=0), jnp.zeros((), BF16))
        att = lax.dot_general(q_stack, k_inv, _NT, preferred_element_type=F32)
        att = jnp.where(causal_rows, att, 0.0).astype(BF16)
        o_all = jnp.dot(att, vv, preferred_element_type=F32)
        o_all = jnp.where(o_rows_mask, o_all, 0.0)
        o = o_all[0:c]
        for h in range(1, nh):
            o = o + o_all[h * c:(h + 1) * c]
        st = st_ref[...]
        o = o + lax.dot_general(qb, st.astype(BF16), _NT, preferred_element_type=F32)
        kv_t = lax.dot_general(vv, k_end, _TN, preferred_element_type=F32)
        st_ref[...] = decay * st + jnp.where(st_mask, kv_t, 0.0)
        outs = []
        for h in range(nh):
            o_h = o[:, h * GLA_DV:(h + 1) * GLA_DV]
            outs.append(o_h * lax.rsqrt(jnp.mean(o_h * o_h, axis=-1, keepdims=True) + NORM_EPS))
        o = jnp.concatenate(outs, axis=-1)
        gg = gg_ref[0, sl, :]
        o_ref[0, sl, :] = o * hg_ref[...] * (gg * jax.nn.sigmoid(gg))


def gla(gq, gk, gv, gg, glr, w_alpha, b_alpha, head_gain, tg=256):
    bsz, t, _ = gq.shape
    wa = jnp.zeros((LANES, GLA_QK_W), F32).at[:GLA_GATE_RANK].set(w_alpha)
    spec = lambda w: pl.BlockSpec((1, tg, w), lambda b, i: (b, i, 0))
    const = lambda r, w: pl.BlockSpec((r, w), lambda b, i: (0, 0))
    return pl.pallas_call(
        functools.partial(_gla_kernel, tg=tg),
        grid=(bsz, t // tg),
        in_specs=[spec(GLA_QK_W), spec(GLA_QK_W), spec(GLA_V_W), spec(GLA_V_W), spec(LANES),
                  const(LANES, GLA_QK_W), const(1, GLA_QK_W), const(1, GLA_V_W)],
        out_specs=spec(GLA_V_W),
        out_shape=jax.ShapeDtypeStruct((bsz, t, GLA_V_W), F32),
        scratch_shapes=[pltpu.VMEM((GLA_V_W, GLA_QK_W), F32)],
        compiler_params=_params("parallel", "arbitrary"),
        name="gla",
    )(gq, gk, gv, gg, glr, wa, b_alpha.reshape(1, -1), head_gain.reshape(1, -1))


def _dilated_kernel(q_ref, kp_ref, kc_ref, vp_ref, vc_ref, o_ref, l_ref, *, dil, blk):
    i = pl.program_id(1)
    cblk = pl.program_id(2)
    n_pairs = q_ref.shape[0]
    n_cblk = ATT_HEADS // 2 // n_pairs
    qi = lax.broadcasted_iota(jnp.int32, (blk, 2 * blk), 0)
    kj = lax.broadcasted_iota(jnp.int32, (blk, 2 * blk), 1)
    delta = qi + blk - kj
    kj_min = jnp.where(i > 0, 0, blk)
    valid = (delta >= 0) & (delta <= blk) & (kj >= kj_min)
    dist = (delta * dil).astype(F32)
    lane = lax.broadcasted_iota(jnp.int32, (1, 2 * ATT_HD), 1)
    lo = lane < ATT_HD

    def alibi(lp, e):
        slopes = [2.0 ** (-8.0 * (2 * (c * n_pairs + lp) + e + 1) / ATT_HEADS) for c in range(n_cblk)]
        slope = jnp.float32(slopes[-1])
        for c in range(n_cblk - 2, -1, -1):
            slope = jnp.where(cblk == c, jnp.float32(slopes[c]), slope)
        return jnp.where(valid, -slope * dist, -1e30)

    biases = [[alibi(lp, e) for e in range(2)] for lp in range(n_pairs)]

    per_trip = max(1, min(dil, ATT_HEADS // (2 * n_pairs)))

    def body(trip, carry):
        for sub in range(per_trip):
            one_class(trip + sub * (dil // per_trip))
        return carry

    def one_class(r):
        rows = pl.ds(r, blk, stride=dil)
        for lp in range(n_pairs):
            q2 = (q_ref[lp, rows, :] * (ATT_HD ** -0.5)).astype(BF16)
            k2 = jnp.concatenate([kp_ref[lp, rows, :], kc_ref[lp, rows, :]], axis=0).astype(BF16)
            v2 = jnp.concatenate([vp_ref[lp, rows, :], vc_ref[lp, rows, :]], axis=0).astype(BF16)
            o_pair = jnp.zeros((blk, 2 * ATT_HD), F32)
            lse_pair = jnp.zeros((blk, 2 * ATT_HD), F32)
            for e in range(2):
                sel = lo if e == 0 else jnp.logical_not(lo)
                qm = jnp.where(sel, q2, jnp.zeros_like(q2))
                vm = jnp.where(sel, v2, jnp.zeros_like(v2))
                s = lax.dot_general(qm, k2, _NT, preferred_element_type=F32) + biases[lp][e]
                m = jnp.max(s, axis=-1, keepdims=True)
                pexp = jnp.exp(s - m)
                l = jnp.sum(pexp, axis=-1, keepdims=True)
                o_e = jnp.dot(pexp.astype(BF16), vm, preferred_element_type=F32)
                o_pair = o_pair + o_e * (1.0 / l)
                lse_pair = jnp.where(sel, m + jnp.log(l), lse_pair)
            o_ref[lp, rows, :] = o_pair
            l_ref[lp, rows, :] = lse_pair

    if dil == per_trip:
        body(0, 0)
    else:
        lax.fori_loop(0, dil // per_trip, body, 0)


def dilated_branch(q, k, v, bsz, window, dil):
    n_slab, m, pw = q.shape
    t = m // bsz
    blk = window // dil
    rows = blk * dil
    assert t % rows == 0 and pw == 2 * ATT_HD
    nblk = t // rows
    n_pairs = n_slab if rows <= 512 else n_slab // 2
    cur = pl.BlockSpec((n_pairs, rows, pw), lambda b, i, c: (c, b * nblk + i, 0))
    prev = pl.BlockSpec((n_pairs, rows, pw), lambda b, i, c: (c, b * nblk + jnp.maximum(i - 1, 0), 0))
    return pl.pallas_call(
        functools.partial(_dilated_kernel, dil=dil, blk=blk),
        grid=(bsz, nblk, n_slab // n_pairs),
        in_specs=[cur, prev, cur, prev, cur],
        out_specs=[cur, cur],
        out_shape=[jax.ShapeDtypeStruct((n_slab, m, pw), F32)] * 2,
        compiler_params=_params("parallel", "arbitrary", "arbitrary"),
        name=f"dilated_d{dil}",
    )(q, k, k, v, v)


def _combine_out_kernel(og_ref, o1_ref, o2_ref, o3_ref, l1_ref, l2_ref, l3_ref, w_ref, r_ref, g2_ref, wq_ref,
                        out_ref, q_ref):
    wide = lambda ref: jnp.concatenate([ref[c] for c in range(ref.shape[0])], axis=-1)
    l1, l2, l3 = wide(l1_ref), wide(l2_ref), wide(l3_ref)
    mx = jnp.maximum(jnp.maximum(l1, l2), l3)
    e1, e2, e3 = jnp.exp(l1 - mx), jnp.exp(l2 - mx), jnp.exp(l3 - mx)
    att = (e1 * wide(o1_ref) + e2 * wide(o2_ref) + e3 * wide(o3_ref)) / (e1 + e2 + e3)
    kg = og_ref.shape[1]
    acc = jnp.dot(og_ref[...].astype(BF16), w_ref[:kg, :], preferred_element_type=F32)
    acc = acc + jnp.dot(att.astype(BF16), w_ref[kg:, :], preferred_element_type=F32)
    h_new = r_ref[...] + acc
    out_ref[...] = h_new
    _query_epilogue(h_new, g2_ref, wq_ref, q_ref)


def combine_out(o_gla, outs, lses, w_bf16, res, gain2, wq_bf16, tm=512):
    m = res.shape[0]
    n = w_bf16.shape[1]
    nq = wq_bf16.shape[1]
    row = lambda w: pl.BlockSpec((tm, w), lambda i: (i, 0))
    slab = pl.BlockSpec((ATT_W // LANES, tm, LANES), lambda i: (0, i, 0))
    return pl.pallas_call(
        _combine_out_kernel,
        grid=(m // tm,),
        in_specs=[row(GLA_V_W)] + [slab] * 6
                 + [pl.BlockSpec(w_bf16.shape, lambda i: (0, 0)), row(n),
                    pl.BlockSpec((1, n), lambda i: (0, 0)), pl.BlockSpec((n, nq), lambda i: (0, 0))],
        out_specs=[row(n), row(nq)],
        out_shape=[jax.ShapeDtypeStruct((m, n), F32), jax.ShapeDtypeStruct((m, nq), F32)],
        compiler_params=_params("parallel"),
        name="combine_out",
    )(o_gla, *outs, *lses, w_bf16, res, gain2.reshape(1, n), wq_bf16)


def _lru_kernel(gate_ref, xr_ref, cw_ref, cb_ref, wr_ref, br_ref, wi_ref, bi_ref, lam_ref, y_ref,
                ext_ref, h_ref, *, tt):
    @pl.when(pl.program_id(1) == 0)
    def _():
        ext_ref[:SUBLANES, :] = jnp.zeros((SUBLANES, ext_ref.shape[1]), F32)
        h_ref[...] = jnp.zeros_like(h_ref)

    xr = xr_ref[0]
    d = xr.shape[1]
    ext_ref[SUBLANES:, :] = xr
    xc = cb_ref[...] + cw_ref[CONV_W - 1:CONV_W, :] * xr
    for j in range(CONV_W - 1):
        off = SUBLANES - (CONV_W - 1) + j
        xc = xc + cw_ref[j:j + 1, :] * ext_ref[off:off + tt, :]
    ext_ref[:SUBLANES, :] = xr[tt - SUBLANES:, :]
    xcb = xc.astype(BF16)
    rs, is_ = [], []
    for g in range(d // RNN_BLOCK_W):
        cs = slice(g * RNN_BLOCK_W, (g + 1) * RNN_BLOCK_W)
        rs.append(jnp.dot(xcb[:, cs], wr_ref[g], preferred_element_type=F32))
        is_.append(jnp.dot(xcb[:, cs], wi_ref[g], preferred_element_type=F32))
    r = jax.nn.sigmoid(jnp.concatenate(rs, axis=-1) + br_ref[...])
    ig = jax.nn.sigmoid(jnp.concatenate(is_, axis=-1) + bi_ref[...])
    lam = lam_ref[...]
    softplus = jnp.maximum(-lam, 0.0) + jnp.log1p(jnp.exp(-jnp.abs(lam)))
    log_a = (-LRU_C) * r * softplus
    a = jnp.exp(log_a)
    th = jnp.tanh(log_a)
    n = -2.0 * th
    u = jnp.where(n > 0.0, n * lax.rsqrt(n), 0.0) * lax.rsqrt(1.0 - th) * (ig * xc)
    a = a.reshape(tt // SUBLANES, SUBLANES, d)
    u = u.reshape(tt // SUBLANES, SUBLANES, d)
    rows = lax.broadcasted_iota(jnp.int32, (1, SUBLANES, 1), 1)
    s = 1
    while s < SUBLANES:
        keep = rows >= s
        a_sh = jnp.where(keep, pltpu.roll(a, s, axis=1), 1.0)
        u_sh = jnp.where(keep, pltpu.roll(u, s, axis=1), 0.0)
        u = u + a * u_sh
        a = a * a_sh
        s *= 2
    carry = h_ref[...]
    gate = gate_ref[0]
    for g in range(tt // SUBLANES):
        hs = u[g] + a[g] * carry
        carry = hs[SUBLANES - 1:SUBLANES, :]
        y_ref[0, g * SUBLANES:(g + 1) * SUBLANES, :] = _gelu_tanh(gate[g * SUBLANES:(g + 1) * SUBLANES]) * hs
    h_ref[...] = carry


def lru(gate, xr, conv_w, conv_b, w_r, b_r, w_i, b_i, lam, tt=256):
    bsz, t, d = xr.shape
    spec = pl.BlockSpec((1, tt, d), lambda b, i: (b, i, 0))
    row = pl.BlockSpec((1, d), lambda b, i: (0, 0))
    wspec = pl.BlockSpec(w_r.shape, lambda b, i: (0, 0, 0))
    return pl.pallas_call(
        functools.partial(_lru_kernel, tt=tt),
        grid=(bsz, t // tt),
        in_specs=[spec, spec, pl.BlockSpec((CONV_W, d), lambda b, i: (0, 0)), row,
                  wspec, row, wspec, row, row],
        out_specs=spec,
        out_shape=jax.ShapeDtypeStruct((bsz, t, d), F32),
        scratch_shapes=[pltpu.VMEM((tt + SUBLANES, d), F32), pltpu.VMEM((1, d), F32)],
        compiler_params=_params("parallel", "arbitrary"),
        name="lru",
    )(gate, xr, conv_w, conv_b.reshape(1, d), w_r.astype(BF16), b_r.reshape(1, d),
      w_i.astype(BF16), b_i.reshape(1, d), lam.reshape(1, d))


def _extract_top(vals, pos, n_rounds, payload=None):
    return _extract_top_many([(vals, pos, payload)], n_rounds)[0]


def _extract_top_many(problems, n_rounds):
    state, outs = [], []
    for vals, pos, payload in problems:
        lo_wins = vals[0] >= vals[1]
        win = lambda x, lo_wins=lo_wins: jnp.where(lo_wins, x[0], x[1])
        lose = lambda x, lo_wins=lo_wins: jnp.where(lo_wins, x[1], x[0])
        st = dict(v_w=win(vals), v_l=lose(vals), p_w=win(pos), p_l=lose(pos))
        if payload is not None:
            st.update(x_w=win(payload), x_l=lose(payload))
        state.append(st)
        outs.append(([], [], []))
    for _ in range(n_rounds):
        for st, (out_v, out_p, out_x) in zip(state, outs):
            m = jnp.max(st["v_w"], axis=0, keepdims=True)
            p = jnp.min(jnp.where(st["v_w"] == m, st["p_w"], 1e9), axis=0, keepdims=True)
            first = st["p_w"] == p
            if "x_w" in st:
                out_x.append(jnp.max(jnp.where(first, st["x_w"], -1.0), axis=0, keepdims=True))
                st["x_w"] = jnp.where(first, st["x_l"], st["x_w"])
            st["v_w"] = jnp.where(first, st["v_l"], st["v_w"])
            st["p_w"] = jnp.where(first, st["p_l"], st["p_w"])
            st["v_l"] = jnp.where(first, -jnp.inf, st["v_l"])
            out_v.append(m)
            out_p.append(p)
    return outs


def _topk_heads(q_ref, keys_ref):
    k = PEER_TOPK
    tm = q_ref.shape[0]
    n_heads = keys_ref.shape[0]
    rows = lambda n: lax.broadcasted_iota(jnp.int32, (n, tm), 0).astype(F32)
    half = PEER_NKEYS // 2
    key_pos = (rows(half), rows(half) + half)
    halves = ([(0, 16), (1, 8), (2, 8), (3, 8)], [(4, 8), (5, 8), (6, 8), (7, 8)])
    id_rows, g_rows = [], []
    for h in range(n_heads):
        scores = []
        for p in range(2):
            keys = keys_ref[h, p]
            qs = q_ref[:, (2 * h + p) * LANES:(2 * h + p + 1) * LANES]
            s_t = tuple(lax.dot_general(keys[kh * half:(kh + 1) * half], qs, _NT, preferred_element_type=F32,
                                        precision=lax.Precision.HIGHEST) for kh in range(2))
            scores.append((s_t, key_pos, None))
        tops = [(jnp.concatenate(v, axis=0), jnp.concatenate(pidx, axis=0))
                for v, pidx, _ in _extract_top_many(scores, k)]
        (t1, i1), (t2, i2) = tops
        cand_v, cand_i, cand_p = [], [], []
        for hi, groups in enumerate(halves):
            cv = [t1[a:a + 1, :] + t2[0:nb, :] for a, nb in groups]
            ci = [i1[a:a + 1, :] * PEER_NKEYS + i2[0:nb, :] for a, nb in groups]
            cp = [a * k + rows(nb) for a, nb in groups]
            if hi == 1:
                cv.append(t1[8:16, :] + t2[0:1, :])
                ci.append(i1[8:16, :] * PEER_NKEYS + i2[0:1, :])
                cp.append((8.0 + rows(8)) * k)
            cand_v.append(jnp.concatenate(cv, axis=0))
            cand_i.append(jnp.concatenate(ci, axis=0))
            cand_p.append(jnp.concatenate(cp, axis=0))
        bv, _, bi = _extract_top(cand_v, cand_p, k, payload=cand_i)
        best = jnp.concatenate(bv, axis=0)
        e = jnp.exp(best - best[0:1, :])
        g_rows.append(e / jnp.sum(e, axis=0, keepdims=True))
        id_rows.append(jnp.concatenate(bi, axis=0))
    return jnp.concatenate(id_rows, axis=0), jnp.concatenate(g_rows, axis=0)


def _peer_topk_kernel(q_ref, keys_ref, ids_ref, g_ref):
    ids_t, g_t = _topk_heads(q_ref, keys_ref)
    ids_ref[...] = ids_t.astype(jnp.int32)
    g_ref[...] = g_t


def peer_topk(q, sub_keys, tm=256):
    m = q.shape[0]
    return pl.pallas_call(
        _peer_topk_kernel,
        grid=(m // tm,),
        in_specs=[pl.BlockSpec((tm, q.shape[1]), lambda i: (i, 0)),
                  pl.BlockSpec(sub_keys.shape, lambda i: (0, 0, 0, 0))],
        out_specs=[pl.BlockSpec((PEER_SLOTS, tm), lambda i: (0, i))] * 2,
        out_shape=[jax.ShapeDtypeStruct((PEER_SLOTS, m), jnp.int32),
                   jax.ShapeDtypeStruct((PEER_SLOTS, m), F32)],
        compiler_params=_params("parallel"),
        name="peer_topk",
    )(q, sub_keys)


def _transpose_sublane_blocks(x):
    x = list(x)
    sub = lax.broadcasted_iota(jnp.int32, x[0].shape, 0)
    s = SUBLANES // 2
    while s >= 1:
        clear = (sub & s) == 0
        for i in range(SUBLANES):
            if i & s:
                continue
            a, b = x[i], x[i + s]
            x[i] = jnp.where(clear, a, pltpu.roll(b, s, axis=0))
            x[i + s] = jnp.where(clear, pltpu.roll(a, SUBLANES - s, axis=0), b)
        s //= 2
    return x


PEER_GROUP = 16


def _peer_onehot_kernel(ids_t_ref, g_t_ref, w_ref, ids_ref, g_ref, ra_ref, rb_ref):
    tm = ids_ref.shape[0]
    n_grp = tm // PEER_GROUP
    for c in range(tm // LANES):
        cs = slice(c * LANES, (c + 1) * LANES)
        ids_ref[cs, :] = ids_t_ref[:, cs].astype(F32).T.astype(jnp.int32)
        g_ref[cs, :] = g_t_ref[:, cs].T
    sub = lax.broadcasted_iota(jnp.int32, (PEER_NKEYS, PEER_SLOTS), 0)
    zeros = jnp.zeros((PEER_NKEYS, PEER_SLOTS), F32)

    def maps(grp, r_ref):
        m0 = pl.multiple_of(grp * PEER_GROUP, PEER_GROUP)
        for t in range(0, PEER_GROUP, 2):
            gates, hots = [], []
            for tok in (t, t + 1):
                ids = ids_ref[pl.ds(m0 + tok, 1), :]
                gates.append(jnp.where(sub == (ids >> 7), g_ref[pl.ds(m0 + tok, 1), :], 0.0))
                hots.append(jnp.where(sub == (ids & (PEER_NKEYS - 1)), 1.0, 0.0))
            lhs = jnp.concatenate([jnp.concatenate([gates[0], zeros], axis=1),
                                   jnp.concatenate([zeros, gates[1]], axis=1)], axis=0)
            bt = jnp.concatenate(hots, axis=1)
            r = lax.dot_general(lhs.astype(BF16), bt.astype(BF16), _NT, preferred_element_type=F32)
            r_ref[t] = r[:PEER_NKEYS]
            r_ref[t + 1] = r[PEER_NKEYS:]

    def emit(grp, r_ref):
        m0 = pl.multiple_of(grp * PEER_GROUP, PEER_GROUP)
        for gi in range(PEER_NKEYS // SUBLANES):
            rows = slice(gi * SUBLANES, (gi + 1) * SUBLANES)
            ya = _transpose_sublane_blocks([r_ref[t, rows, :] for t in range(SUBLANES)])
            yb = _transpose_sublane_blocks([r_ref[SUBLANES + t, rows, :] for t in range(SUBLANES)])
            for r in range(SUBLANES):
                tile = jnp.concatenate([ya[r], yb[r]], axis=0).astype(BF16)
                w_ref[gi * SUBLANES + r, pl.ds(m0, PEER_GROUP), :] = tile

    maps(0, ra_ref)

    def body(k, carry):
        g0 = 2 * k
        maps(g0 + 1, rb_ref)
        emit(g0, ra_ref)
        maps(g0 + 2, ra_ref)
        emit(g0 + 1, rb_ref)
        return carry

    lax.fori_loop(0, n_grp // 2 - 1, body, 0)
    maps(n_grp - 1, rb_ref)
    emit(n_grp - 2, ra_ref)
    emit(n_grp - 1, rb_ref)


def peer_onehot(ids_t, g_t, tm=256):
    m = ids_t.shape[1]
    assert tm % (2 * PEER_GROUP) == 0 and tm % LANES == 0
    res = pltpu.VMEM((PEER_GROUP, PEER_NKEYS, PEER_NKEYS), F32)
    return pl.pallas_call(
        _peer_onehot_kernel,
        grid=(m // tm,),
        in_specs=[pl.BlockSpec((PEER_SLOTS, tm), lambda i: (0, i))] * 2,
        out_specs=pl.BlockSpec((PEER_NKEYS, tm, PEER_NKEYS), lambda i: (0, i, 0)),
        out_shape=jax.ShapeDtypeStruct((PEER_NKEYS, m, PEER_NKEYS), BF16),
        scratch_shapes=[pltpu.VMEM((tm, PEER_SLOTS), jnp.int32), pltpu.VMEM((tm, PEER_SLOTS), F32), res, res],
        compiler_params=_params("parallel"),
        name="peer_onehot",
    )(ids_t, g_t)


def _peer_dense_kernel(h_ref, gain_ref, w_ref, u_ref, v_ref, fg_ref, o_ref, hn_ref, acc_ref, *,
                       groups, final_norm):
    j = pl.program_id(1)

    @pl.when(j == 0)
    def _():
        hn_ref[...] = _rms(h_ref[...], gain_ref[...]).astype(BF16)
        acc_ref[...] = jnp.zeros_like(acc_ref)

    z = jnp.dot(hn_ref[...], u_ref[...], preferred_element_type=F32)
    wsel = jnp.concatenate([w_ref[gi] for gi in range(groups)], axis=-1)
    act = _gelu_tanh(z).astype(BF16) * wsel
    acc_ref[...] += jnp.dot(act, v_ref[...], preferred_element_type=F32)

    @pl.when(j == pl.num_programs(1) - 1)
    def _():
        out = h_ref[...] + acc_ref[...]
        if final_norm:
            out = _rms(out, fg_ref[...])
        o_ref[...] = out


def peer_dense(h, gain, wmap, ut_all, v_all, layer, final_gain, final_norm, tm=512, groups=16):
    m, d = h.shape
    ne = v_all.shape[1]
    te = groups * PEER_NKEYS
    return pl.pallas_call(
        functools.partial(_peer_dense_kernel, groups=groups, final_norm=final_norm),
        grid=(m // tm, ne // te),
        in_specs=[pl.BlockSpec((tm, d), lambda i, j: (i, 0)),
                  pl.BlockSpec((1, d), lambda i, j: (0, 0)),
                  pl.BlockSpec((groups, tm, PEER_NKEYS), lambda i, j: (j, i, 0)),
                  pl.BlockSpec((None, d, te), lambda i, j: (layer, 0, j)),
                  pl.BlockSpec((None, te, d), lambda i, j: (layer, j, 0)),
                  pl.BlockSpec((1, d), lambda i, j: (0, 0))],
        out_specs=pl.BlockSpec((tm, d), lambda i, j: (i, 0)),
        out_shape=jax.ShapeDtypeStruct((m, d), F32),
        scratch_shapes=[pltpu.VMEM((tm, d), BF16), pltpu.VMEM((tm, d), F32)],
        compiler_params=_params("parallel", "arbitrary"),
        name="peer_dense",
    )(h, gain.reshape(1, d), wmap, ut_all, v_all, final_gain.reshape(1, d))


def peer_ffn(h, q, gain, sub_keys, ut_all, v_all, layer, final_gain, final_norm):
    ids, g = peer_topk(q, sub_keys)
    wmap = peer_onehot(ids, g)
    return peer_dense(h, gain, wmap, ut_all, v_all, layer, final_gain, final_norm)


def gla_dilated_layer(h, bsz, gain, w_in, w_alpha, b_alpha, head_gain, w_out, ffn_gain, w_query):
    m, d = h.shape
    t = m // bsz
    sizes = [GLA_QK_W, GLA_QK_W, GLA_V_W, GLA_V_W, GLA_GATE_RANK, ATT_W, ATT_W, ATT_W]
    offs = [0]
    for s in sizes:
        offs.append(offs[-1] + s)
    seg = lambda i: w_in[:, offs[i]:offs[i + 1]]
    pad = jnp.zeros((d, LANES - GLA_GATE_RANK), w_in.dtype)
    w_cat = jnp.concatenate([seg(0), seg(1), seg(2), seg(3), seg(5), seg(6), seg(7), seg(4), pad], axis=1)
    widths = [GLA_QK_W, GLA_QK_W, GLA_V_W, GLA_V_W, ATT_W, ATT_W, ATT_W, LANES]
    dts = [F32] * len(widths)
    gq, gk, gv, gg, aq, ak, av, glr = norm_matmul(h, gain, w_cat.astype(BF16), widths, dts,
                                                  lane_major=(4, 5, 6))
    r3 = lambda a: a.reshape(bsz, t, a.shape[1])
    o_gla = gla(r3(gq), r3(gk), r3(gv), r3(gg), r3(glr), w_alpha, b_alpha, head_gain)
    outs, lses = [], []
    for window, dil in DILATED_BRANCHES:
        o, l = dilated_branch(aq, ak, av, bsz, window, dil)
        outs.append(o)
        lses.append(l)
    return combine_out(o_gla.reshape(m, GLA_V_W), outs, lses, w_out.astype(BF16), h, ffn_gain,
                       w_query.astype(BF16))


def rglru_layer(h, bsz, gain, w_in, conv_w, conv_b, w_r, b_r, w_i, b_i, lam, w_out, ffn_gain, w_query):
    m, d = h.shape
    t = m // bsz
    d_rnn = w_in.shape[1] // 2
    gate, xr = norm_matmul(h, gain, w_in.astype(BF16), [d_rnn, d_rnn], [F32, F32])
    y = lru(gate.reshape(bsz, t, d_rnn), xr.reshape(bsz, t, d_rnn), conv_w, conv_b, w_r, b_r, w_i, b_i, lam)
    return matmul_res(y.reshape(m, d_rnn), w_out.astype(BF16), h, ffn_gain, w_query.astype(BF16))


def kernel(x, mix_norm, ffn_norm, ab_w_in, gla_w_alpha, gla_b_alpha, gla_head_gain, ab_w_out,
           c_w_in, c_conv_w, c_conv_b, c_w_rgate, c_b_rgate, c_w_igate, c_b_igate, c_lambda, c_w_out,
           peer_w_query, peer_sub_keys, peer_u, peer_v, final_norm):
    bsz, t, d = x.shape
    depth = mix_norm.shape[0]
    h = x.reshape(bsz * t, d)
    ut_all = peer_u.astype(BF16).transpose(0, 2, 1)
    v_all = peer_v.astype(BF16)
    for layer in range(depth):
        if layer % 2 == 0:
            e = layer // 2
            h, q = gla_dilated_layer(h, bsz, mix_norm[layer], ab_w_in[e], gla_w_alpha[e], gla_b_alpha[e],
                                     gla_head_gain[e], ab_w_out[e], ffn_norm[layer], peer_w_query[layer])
        else:
            o = layer // 2
            h, q = rglru_layer(h, bsz, mix_norm[layer], c_w_in[o], c_conv_w[o], c_conv_b[o], c_w_rgate[o],
                               c_b_rgate[o], c_w_igate[o], c_b_igate[o], c_lambda[o], c_w_out[o],
                               ffn_norm[layer], peer_w_query[layer])
        h = peer_ffn(h, q, ffn_norm[layer], peer_sub_keys[layer], ut_all, v_all, layer,
                     final_norm, final_norm=(layer == depth - 1))
    return h.reshape(bsz, t, d)
```

```python
import functools
import math

import jax
import jax.numpy as jnp
from jax import lax
from jax.experimental import pallas as pl
from jax.experimental.pallas import tpu as pltpu

F32 = jnp.float32
BF16 = jnp.bfloat16

NORM_EPS = 1e-6
V7X_VMEM_LIMIT = 56 * 1024 * 1024

GLA_HEADS = 4
GLA_DK = 64
GLA_DV = 128
GLA_GATE_RANK = 16
GLA_GATE_TAU = 16.0
GLA_CHUNK = 64
ATT_HEADS = 8
ATT_HD = 64
DILATED_BRANCHES = ((128, 1), (512, 4), (2048, 16))
GLA_QK_W = GLA_HEADS * GLA_DK
GLA_V_W = GLA_HEADS * GLA_DV
ATT_W = ATT_HEADS * ATT_HD
CONV_W = 4
LRU_C = 8.0
RNN_BLOCK_W = 128
PEER_HEADS = 8
PEER_NKEYS = 128
PEER_TOPK = 16
PEER_SLOTS = PEER_HEADS * PEER_TOPK
LANES = 128
SUBLANES = 8

_NT = (((1,), (1,)), ((), ()))
_TN = (((0,), (0,)), ((), ()))


def _params(*sem):
    return pltpu.CompilerParams(dimension_semantics=sem, vmem_limit_bytes=V7X_VMEM_LIMIT)


def _gelu_tanh(x):
    c = math.sqrt(2.0 / math.pi)
    return (0.5 * x) * (1.0 + jnp.tanh(x * (c + (0.044715 * c) * (x * x))))


def _rms(x, gain):
    ms = jnp.mean(x * x, axis=-1, keepdims=True)
    return x * lax.rsqrt(ms + NORM_EPS) * gain


def _norm_matmul_kernel(x_ref, g_ref, w_ref, *o_refs, widths):
    xn = _rms(x_ref[...], g_ref[...]).astype(BF16)
    off = 0
    for o_ref, w in zip(o_refs, widths):
        y = jnp.dot(xn, w_ref[:, off:off + w], preferred_element_type=F32).astype(o_ref.dtype)
        if len(o_ref.shape) == 3:
            for c in range(w // LANES):
                o_ref[c] = y[:, c * LANES:(c + 1) * LANES]
        else:
            o_ref[...] = y
        off += w


def norm_matmul(x, gain, w_bf16, widths, dtypes, tm=512, lane_major=()):
    m, d = x.shape
    n = w_bf16.shape[1]
    assert sum(widths) == n and m % tm == 0
    out_specs, out_shape = [], []
    for k, (w, dt) in enumerate(zip(widths, dtypes)):
        if k in lane_major:
            out_specs.append(pl.BlockSpec((w // LANES, tm, LANES), lambda i: (0, i, 0)))
            out_shape.append(jax.ShapeDtypeStruct((w // LANES, m, LANES), dt))
        else:
            out_specs.append(pl.BlockSpec((tm, w), lambda i: (i, 0)))
            out_shape.append(jax.ShapeDtypeStruct((m, w), dt))
    return pl.pallas_call(
        functools.partial(_norm_matmul_kernel, widths=tuple(widths)),
        grid=(m // tm,),
        in_specs=[pl.BlockSpec((tm, d), lambda i: (i, 0)),
                  pl.BlockSpec((1, d), lambda i: (0, 0)),
                  pl.BlockSpec((d, n), lambda i: (0, 0))],
        out_specs=out_specs,
        out_shape=out_shape,
        compiler_params=_params("parallel"),
        name="norm_matmul",
    )(x, gain.reshape(1, d), w_bf16)


def _query_epilogue(h_new, g2_ref, wq_ref, q_ref):
    hn = _rms(h_new, g2_ref[...]).astype(BF16)
    q_ref[...] = jnp.dot(hn, wq_ref[...], preferred_element_type=F32)


def _matmul_res_kernel(x_ref, w_ref, r_ref, g2_ref, wq_ref, o_ref, q_ref):
    h_new = r_ref[...] + jnp.dot(x_ref[...].astype(BF16), w_ref[...], preferred_element_type=F32)
    o_ref[...] = h_new
    _query_epilogue(h_new, g2_ref, wq_ref, q_ref)


def matmul_res(x, w_bf16, res, gain2, wq_bf16, tm=512):
    m, k = x.shape
    n = w_bf16.shape[1]
    nq = wq_bf16.shape[1]
    return pl.pallas_call(
        _matmul_res_kernel,
        grid=(m // tm,),
        in_specs=[pl.BlockSpec((tm, k), lambda i: (i, 0)),
                  pl.BlockSpec((k, n), lambda i: (0, 0)),
                  pl.BlockSpec((tm, n), lambda i: (i, 0)),
                  pl.BlockSpec((1, n), lambda i: (0, 0)),
                  pl.BlockSpec((n, nq), lambda i: (0, 0))],
        out_specs=[pl.BlockSpec((tm, n), lambda i: (i, 0)), pl.BlockSpec((tm, nq), lambda i: (i, 0))],
        out_shape=[jax.ShapeDtypeStruct((m, n), F32), jax.ShapeDtypeStruct((m, nq), F32)],
        compiler_params=_params("parallel"),
        name="matmul_res",
    )(x, w_bf16, res, gain2.reshape(1, n), wq_bf16)


def _gla_kernel(q_ref, k_ref, v_ref, gg_ref, lr_ref, wa_ref, ba_ref, hg_ref, o_ref, st_ref, *, tg):
    c = GLA_CHUNK

    @pl.when(pl.program_id(1) == 0)
    def _():
        st_ref[...] = jnp.zeros_like(st_ref)

    z = jnp.dot(lr_ref[0], wa_ref[...], preferred_element_type=F32,
                precision=lax.Precision.HIGHEST) + ba_ref[...]
    log_a = (jnp.minimum(z, 0.0) - jnp.log1p(jnp.exp(-jnp.abs(z)))) * (1.0 / GLA_GATE_TAU)
    nh = GLA_HEADS
    tril = (lax.broadcasted_iota(jnp.int32, (c, c), 0) >= lax.broadcasted_iota(jnp.int32, (c, c), 1)).astype(F32)
    klane_head = lax.broadcasted_iota(jnp.int32, (1, GLA_QK_W), 1) // GLA_DK
    vlane_head = lax.broadcasted_iota(jnp.int32, (1, GLA_V_W), 1) // GLA_DV
    row_head = lax.broadcasted_iota(jnp.int32, (nh * c, 1), 0) // c
    row_t = lax.broadcasted_iota(jnp.int32, (nh * c, c), 0) % c
    causal_rows = row_t >= lax.broadcasted_iota(jnp.int32, (nh * c, c), 1)
    q_rows_mask = row_head == klane_head
    o_rows_mask = row_head == vlane_head
    st_mask = (lax.broadcasted_iota(jnp.int32, (GLA_V_W, 1), 0) // GLA_DV) == klane_head
    for ci in range(tg // c):
        sl = slice(ci * c, (ci + 1) * c)
        b = jnp.dot(tril, log_a[sl], preferred_element_type=F32, precision=lax.Precision.HIGHEST)
        b_last = b[c - 1:c, :]
        q_dec = q_ref[0, sl, :] * (GLA_DK ** -0.5) * jnp.exp(b)
        kk = k_ref[0, sl, :]
        k_inv = (kk * jnp.exp(-b)).astype(BF16)
        k_end = (kk * jnp.exp(b_last - b)).astype(BF16)
        decay = jnp.exp(b_last)
        vv = v_ref[0, sl, :].astype(BF16)
        qb = q_dec.astype(BF16)
        q_stack = jnp.where(q_rows_mask, jnp.concatenate([qb] * nh, axis=0), jnp.zeros((), BF16))
        att = lax.dot_general(q_stack, k_inv, _NT, preferred_element_type=F32)
        att = jnp.where(causal_rows, att, 0.0).astype(BF16)
        o_all = jnp.dot(att, vv, preferred_element_type=F32)
        o_all = jnp.where(o_rows_mask, o_all, 0.0)
        o = o_all[0:c]
        for h in range(1, nh):
            o = o + o_all[h * c:(h + 1) * c]
        st = st_ref[...]
        o = o + lax.dot_general(qb, st.astype(BF16), _NT, preferred_element_type=F32)
        kv_t = lax.dot_general(vv, k_end, _TN, preferred_element_type=F32)
        st_ref[...] = decay * st + jnp.where(st_mask, kv_t, 0.0)
        outs = []
        for h in range(nh):
            o_h = o[:, h * GLA_DV:(h + 1) * GLA_DV]
            outs.append(o_h * lax.rsqrt(jnp.mean(o_h * o_h, axis=-1, keepdims=True) + NORM_EPS))
        o = jnp.concatenate(outs, axis=-1)
        gg = gg_ref[0, sl, :]
        o_ref[0, sl, :] = o * hg_ref[...] * (gg * jax.nn.sigmoid(gg))


def gla(gq, gk, gv, gg, glr, w_alpha, b_alpha, head_gain, tg=256):
    bsz, t, _ = gq.shape
    wa = jnp.zeros((LANES, GLA_QK_W), F32).at[:GLA_GATE_RANK].set(w_alpha)
    spec = lambda w: pl.BlockSpec((1, tg, w), lambda b, i: (b, i, 0))
    const = lambda r, w: pl.BlockSpec((r, w), lambda b, i: (0, 0))
    return pl.pallas_call(
        functools.partial(_gla_kernel, tg=tg),
        grid=(bsz, t // tg),
        in_specs=[spec(GLA_QK_W), spec(GLA_QK_W), spec(GLA_V_W), spec(GLA_V_W), spec(LANES),
                  const(LANES, GLA_QK_W), const(1, GLA_QK_W), const(1, GLA_V_W)],
        out_specs=spec(GLA_V_W),
        out_shape=jax.ShapeDtypeStruct((bsz, t, GLA_V_W), F32),
        scratch_shapes=[pltpu.VMEM((GLA_V_W, GLA_QK_W), F32)],
        compiler_params=_params("parallel", "arbitrary"),
        name="gla",
    )(gq, gk, gv, gg, glr, wa, b_alpha.reshape(1, -1), head_gain.reshape(1, -1))


def _dilated_kernel(q_ref, kp_ref, kc_ref, vp_ref, vc_ref, o_ref, l_ref, *, dil, blk):
    i = pl.program_id(1)
    cblk = pl.program_id(2)
    n_pairs = q_ref.shape[0]
    n_cblk = ATT_HEADS // 2 // n_pairs
    n_sub = q_ref.shape[1] // (blk * dil)
    qi = lax.broadcasted_iota(jnp.int32, (blk, 2 * blk), 0)
    kj = lax.broadcasted_iota(jnp.int32, (blk, 2 * blk), 1)
    delta = qi + blk - kj
    band = (delta >= 0) & (delta <= blk)
    valid = [band & (kj >= jnp.where(i > 0, 0, blk))] + [band] * (n_sub - 1)
    dist = (delta * dil).astype(F32)
    lane = lax.broadcasted_iota(jnp.int32, (1, 2 * ATT_HD), 1)
    lo = lane < ATT_HD

    def alibi(lp, e, ok):
        slopes = [2.0 ** (-8.0 * (2 * (c * n_pairs + lp) + e + 1) / ATT_HEADS) for c in range(n_cblk)]
        slope = jnp.float32(slopes[-1])
        for c in range(n_cblk - 2, -1, -1):
            slope = jnp.where(cblk == c, jnp.float32(slopes[c]), slope)
        return jnp.where(ok, -slope * dist, -1e30)

    biases = [[[alibi(lp, e, ok) for e in range(2)] for lp in range(n_pairs)] for ok in valid[:2]]

    per_trip = max(1, min(dil, 2 * ATT_HEADS // (2 * n_pairs * n_sub)))

    def body(trip, carry):
        for sub in range(per_trip):
            one_class(trip + sub * (dil // per_trip))
        return carry

    def one_class(r):
        for sb in range(n_sub):
            rows = pl.ds(r + sb * blk * dil, blk, stride=dil)
            if sb == 0:
                p_ref_k, p_ref_v, prows = kp_ref, vp_ref, pl.ds(r + (n_sub - 1) * blk * dil, blk, stride=dil)
            else:
                p_ref_k, p_ref_v, prows = kc_ref, vc_ref, pl.ds(r + (sb - 1) * blk * dil, blk, stride=dil)
            bias = biases[min(sb, 1)]
            for lp in range(n_pairs):
                q2 = (q_ref[lp, rows, :] * (ATT_HD ** -0.5)).astype(BF16)
                k2 = jnp.concatenate([p_ref_k[lp, prows, :], kc_ref[lp, rows, :]], axis=0).astype(BF16)
                v2 = jnp.concatenate([p_ref_v[lp, prows, :], vc_ref[lp, rows, :]], axis=0).astype(BF16)
                o_pair = jnp.zeros((blk, 2 * ATT_HD), F32)
                lse_pair = jnp.zeros((blk, 2 * ATT_HD), F32)
                for e in range(2):
                    sel = lo if e == 0 else jnp.logical_not(lo)
                    qm = jnp.where(sel, q2, jnp.zeros_like(q2))
                    vm = jnp.where(sel, v2, jnp.zeros_like(v2))
                    s = lax.dot_general(qm, k2, _NT, preferred_element_type=F32) + bias[lp][e]
                    m = jnp.max(s, axis=-1, keepdims=True)
                    pexp = jnp.exp(s - m)
                    l = jnp.sum(pexp, axis=-1, keepdims=True)
                    o_e = jnp.dot(pexp.astype(BF16), vm, preferred_element_type=F32)
                    o_pair = o_pair + o_e * (1.0 / l)
                    lse_pair = jnp.where(sel, m + jnp.log(l), lse_pair)
                o_ref[lp, rows, :] = o_pair
                l_ref[lp, rows, :] = lse_pair

    if dil == per_trip:
        body(0, 0)
    else:
        lax.fori_loop(0, dil // per_trip, body, 0)


def dilated_branch(q, k, v, bsz, window, dil):
    n_slab, m, pw = q.shape
    t = m // bsz
    blk = window // dil
    rows = blk * dil * (2 if blk * dil <= LANES else 1)
    assert t % rows == 0 and pw == 2 * ATT_HD
    nblk = t // rows
    n_pairs = n_slab if rows <= 512 else n_slab // 2
    cur = pl.BlockSpec((n_pairs, rows, pw), lambda b, i, c: (c, b * nblk + i, 0))
    prev = pl.BlockSpec((n_pairs, rows, pw), lambda b, i, c: (c, b * nblk + jnp.maximum(i - 1, 0), 0))
    return pl.pallas_call(
        functools.partial(_dilated_kernel, dil=dil, blk=blk),
        grid=(bsz, nblk, n_slab // n_pairs),
        in_specs=[cur, prev, cur, prev, cur],
        out_specs=[cur, cur],
        out_shape=[jax.ShapeDtypeStruct((n_slab, m, pw), F32)] * 2,
        compiler_params=_params("parallel", "arbitrary", "arbitrary"),
        name=f"dilated_d{dil}",
    )(q, k, k, v, v)


def _combine_out_kernel(og_ref, o1_ref, o2_ref, o3_ref, l1_ref, l2_ref, l3_ref, w_ref, r_ref, g2_ref, wq_ref,
                        out_ref, q_ref):
    wide = lambda ref: jnp.concatenate([ref[c] for c in range(ref.shape[0])], axis=-1)
    l1, l2, l3 = wide(l1_ref), wide(l2_ref), wide(l3_ref)
    mx = jnp.maximum(jnp.maximum(l1, l2), l3)
    e1, e2, e3 = jnp.exp(l1 - mx), jnp.exp(l2 - mx), jnp.exp(l3 - mx)
    att = (e1 * wide(o1_ref) + e2 * wide(o2_ref) + e3 * wide(o3_ref)) / (e1 + e2 + e3)
    kg = og_ref.shape[1]
    acc = jnp.dot(og_ref[...].astype(BF16), w_ref[:kg, :], preferred_element_type=F32)
    acc = acc + jnp.dot(att.astype(BF16), w_ref[kg:, :], preferred_element_type=F32)
    h_new = r_ref[...] + acc
    out_ref[...] = h_new
    _query_epilogue(h_new, g2_ref, wq_ref, q_ref)


def combine_out(o_gla, outs, lses, w_bf16, res, gain2, wq_bf16, tm=512):
    m = res.shape[0]
    n = w_bf16.shape[1]
    nq = wq_bf16.shape[1]
    row = lambda w: pl.BlockSpec((tm, w), lambda i: (i, 0))
    slab = pl.BlockSpec((ATT_W // LANES, tm, LANES), lambda i: (0, i, 0))
    return pl.pallas_call(
        _combine_out_kernel,
        grid=(m // tm,),
        in_specs=[row(GLA_V_W)] + [slab] * 6
                 + [pl.BlockSpec(w_bf16.shape, lambda i: (0, 0)), row(n),
                    pl.BlockSpec((1, n), lambda i: (0, 0)), pl.BlockSpec((n, nq), lambda i: (0, 0))],
        out_specs=[row(n), row(nq)],
        out_shape=[jax.ShapeDtypeStruct((m, n), F32), jax.ShapeDtypeStruct((m, nq), F32)],
        compiler_params=_params("parallel"),
        name="combine_out",
    )(o_gla, *outs, *lses, w_bf16, res, gain2.reshape(1, n), wq_bf16)


def _lru_kernel(gate_ref, xr_ref, cw_ref, cb_ref, wr_ref, br_ref, wi_ref, bi_ref, lam_ref, y_ref,
                ext_ref, h_ref, *, tt):
    @pl.when(pl.program_id(1) == 0)
    def _():
        ext_ref[:SUBLANES, :] = jnp.zeros((SUBLANES, ext_ref.shape[1]), F32)
        h_ref[...] = jnp.zeros_like(h_ref)

    xr = xr_ref[0]
    d = xr.shape[1]
    ext_ref[SUBLANES:, :] = xr
    xc = cb_ref[...] + cw_ref[CONV_W - 1:CONV_W, :] * xr
    for j in range(CONV_W - 1):
        off = SUBLANES - (CONV_W - 1) + j
        xc = xc + cw_ref[j:j + 1, :] * ext_ref[off:off + tt, :]
    ext_ref[:SUBLANES, :] = xr[tt - SUBLANES:, :]
    xcb = xc.astype(BF16)
    rs, is_ = [], []
    for g in range(d // RNN_BLOCK_W):
        cs = slice(g * RNN_BLOCK_W, (g + 1) * RNN_BLOCK_W)
        rs.append(jnp.dot(xcb[:, cs], wr_ref[g], preferred_element_type=F32))
        is_.append(jnp.dot(xcb[:, cs], wi_ref[g], preferred_element_type=F32))
    r = jax.nn.sigmoid(jnp.concatenate(rs, axis=-1) + br_ref[...])
    ig = jax.nn.sigmoid(jnp.concatenate(is_, axis=-1) + bi_ref[...])
    lam = lam_ref[...]
    softplus = jnp.maximum(-lam, 0.0) + jnp.log1p(jnp.exp(-jnp.abs(lam)))
    log_a = (-LRU_C) * r * softplus
    a = jnp.exp(log_a)
    th = jnp.tanh(log_a)
    n = -2.0 * th
    u = jnp.where(n > 0.0, n * lax.rsqrt(n), 0.0) * lax.rsqrt(1.0 - th) * (ig * xc)
    a = a.reshape(tt // SUBLANES, SUBLANES, d)
    u = u.reshape(tt // SUBLANES, SUBLANES, d)
    rows = lax.broadcasted_iota(jnp.int32, (1, SUBLANES, 1), 1)
    s = 1
    while s < SUBLANES:
        keep = rows >= s
        a_sh = jnp.where(keep, pltpu.roll(a, s, axis=1), 1.0)
        u_sh = jnp.where(keep, pltpu.roll(u, s, axis=1), 0.0)
        u = u + a * u_sh
        a = a * a_sh
        s *= 2
    carry = h_ref[...]
    gate = gate_ref[0]
    for g in range(tt // SUBLANES):
        hs = u[g] + a[g] * carry
        carry = hs[SUBLANES - 1:SUBLANES, :]
        y_ref[0, g * SUBLANES:(g + 1) * SUBLANES, :] = _gelu_tanh(gate[g * SUBLANES:(g + 1) * SUBLANES]) * hs
    h_ref[...] = carry


def lru(gate, xr, conv_w, conv_b, w_r, b_r, w_i, b_i, lam, tt=256):
    bsz, t, d = xr.shape
    spec = pl.BlockSpec((1, tt, d), lambda b, i: (b, i, 0))
    row = pl.BlockSpec((1, d), lambda b, i: (0, 0))
    wspec = pl.BlockSpec(w_r.shape, lambda b, i: (0, 0, 0))
    return pl.pallas_call(
        functools.partial(_lru_kernel, tt=tt),
        grid=(bsz, t // tt),
        in_specs=[spec, spec, pl.BlockSpec((CONV_W, d), lambda b, i: (0, 0)), row,
                  wspec, row, wspec, row, row],
        out_specs=spec,
        out_shape=jax.ShapeDtypeStruct((bsz, t, d), F32),
        scratch_shapes=[pltpu.VMEM((tt + SUBLANES, d), F32), pltpu.VMEM((1, d), F32)],
        compiler_params=_params("parallel", "arbitrary"),
        name="lru",
    )(gate, xr, conv_w, conv_b.reshape(1, d), w_r.astype(BF16), b_r.reshape(1, d),
      w_i.astype(BF16), b_i.reshape(1, d), lam.reshape(1, d))


def _extract_top(vals, pos, n_rounds, payload=None):
    return _extract_top_many([(vals, pos, payload)], n_rounds)[0]


def _extract_top_many(problems, n_rounds):
    state, outs = [], []
    for vals, pos, payload in problems:
        lo_wins = vals[0] >= vals[1]
        win = lambda x, lo_wins=lo_wins: jnp.where(lo_wins, x[0], x[1])
        lose = lambda x, lo_wins=lo_wins: jnp.where(lo_wins, x[1], x[0])
        st = dict(v_w=win(vals), v_l=lose(vals), p_w=win(pos), p_l=lose(pos))
        if payload is not None:
            st.update(x_w=win(payload), x_l=lose(payload))
        state.append(st)
        outs.append(([], [], []))
    for _ in range(n_rounds):
        for st, (out_v, out_p, out_x) in zip(state, outs):
            m = jnp.max(st["v_w"], axis=0, keepdims=True)
            p = jnp.min(jnp.where(st["v_w"] == m, st["p_w"], 1e9), axis=0, keepdims=True)
            first = st["p_w"] == p
            if "x_w" in st:
                out_x.append(jnp.max(jnp.where(first, st["x_w"], -1.0), axis=0, keepdims=True))
                st["x_w"] = jnp.where(first, st["x_l"], st["x_w"])
            st["v_w"] = jnp.where(first, st["v_l"], st["v_w"])
            st["p_w"] = jnp.where(first, st["p_l"], st["p_w"])
            st["v_l"] = jnp.where(first, -jnp.inf, st["v_l"])
            out_v.append(m)
            out_p.append(p)
    return outs


def _topk_heads(q_ref, keys_ref):
    k = PEER_TOPK
    tm = q_ref.shape[0]
    n_heads = keys_ref.shape[0]
    rows = lambda n: lax.broadcasted_iota(jnp.int32, (n, tm), 0).astype(F32)
    half = PEER_NKEYS // 2
    key_pos = (rows(half), rows(half) + half)
    halves = ([(0, 16), (1, 8), (2, 8), (3, 8)], [(4, 8), (5, 8), (6, 8), (7, 8)])
    id_rows, g_rows = [], []
    for h in range(n_heads):
        scores = []
        for p in range(2):
            keys = keys_ref[h, p]
            qs = q_ref[:, (2 * h + p) * LANES:(2 * h + p + 1) * LANES]
            s_t = tuple(lax.dot_general(keys[kh * half:(kh + 1) * half], qs, _NT, preferred_element_type=F32,
                                        precision=lax.Precision.HIGHEST) for kh in range(2))
            scores.append((s_t, key_pos, None))
        tops = [(jnp.concatenate(v, axis=0), jnp.concatenate(pidx, axis=0))
                for v, pidx, _ in _extract_top_many(scores, k)]
        (t1, i1), (t2, i2) = tops
        cand_v, cand_i, cand_p = [], [], []
        for hi, groups in enumerate(halves):
            cv = [t1[a:a + 1, :] + t2[0:nb, :] for a, nb in groups]
            ci = [i1[a:a + 1, :] * PEER_NKEYS + i2[0:nb, :] for a, nb in groups]
            cp = [a * k + rows(nb) for a, nb in groups]
            if hi == 1:
                cv.append(t1[8:16, :] + t2[0:1, :])
                ci.append(i1[8:16, :] * PEER_NKEYS + i2[0:1, :])
                cp.append((8.0 + rows(8)) * k)
            cand_v.append(jnp.concatenate(cv, axis=0))
            cand_i.append(jnp.concatenate(ci, axis=0))
            cand_p.append(jnp.concatenate(cp, axis=0))
        bv, _, bi = _extract_top(cand_v, cand_p, k, payload=cand_i)
        best = jnp.concatenate(bv, axis=0)
        e = jnp.exp(best - best[0:1, :])
        g_rows.append(e / jnp.sum(e, axis=0, keepdims=True))
        id_rows.append(jnp.concatenate(bi, axis=0))
    return jnp.concatenate(id_rows, axis=0), jnp.concatenate(g_rows, axis=0)


def _peer_topk_kernel(q_ref, keys_ref, ids_ref, g_ref):
    ids_t, g_t = _topk_heads(q_ref, keys_ref)
    ids_ref[...] = ids_t.astype(jnp.int32)
    g_ref[...] = g_t


def peer_topk(q, sub_keys, tm=256):
    m = q.shape[0]
    return pl.pallas_call(
        _peer_topk_kernel,
        grid=(m // tm,),
        in_specs=[pl.BlockSpec((tm, q.shape[1]), lambda i: (i, 0)),
                  pl.BlockSpec(sub_keys.shape, lambda i: (0, 0, 0, 0))],
        out_specs=[pl.BlockSpec((PEER_SLOTS, tm), lambda i: (0, i))] * 2,
        out_shape=[jax.ShapeDtypeStruct((PEER_SLOTS, m), jnp.int32),
                   jax.ShapeDtypeStruct((PEER_SLOTS, m), F32)],
        compiler_params=_params("parallel"),
        name="peer_topk",
    )(q, sub_keys)


def _transpose_sublane_blocks(x):
    x = list(x)
    sub = lax.broadcasted_iota(jnp.int32, x[0].shape, 0)
    s = SUBLANES // 2
    while s >= 1:
        clear = (sub & s) == 0
        for i in range(SUBLANES):
            if i & s:
                continue
            a, b = x[i], x[i + s]
            x[i] = jnp.where(clear, a, pltpu.roll(b, s, axis=0))
            x[i + s] = jnp.where(clear, pltpu.roll(a, SUBLANES - s, axis=0), b)
        s //= 2
    return x


PEER_GROUP = 16


def _peer_onehot_kernel(ids_t_ref, g_t_ref, w_ref, ids_ref, g_ref, ra_ref, rb_ref):
    tm = ids_ref.shape[0]
    n_grp = tm // PEER_GROUP
    for c in range(tm // LANES):
        cs = slice(c * LANES, (c + 1) * LANES)
        ids_ref[cs, :] = ids_t_ref[:, cs].astype(F32).T.astype(jnp.int32)
        g_ref[cs, :] = g_t_ref[:, cs].T
    sub = lax.broadcasted_iota(jnp.int32, (PEER_NKEYS, PEER_SLOTS), 0)
    zeros = jnp.zeros((PEER_NKEYS, PEER_SLOTS), F32)

    def maps(grp, r_ref):
        m0 = pl.multiple_of(grp * PEER_GROUP, PEER_GROUP)
        for t in range(0, PEER_GROUP, 2):
            gates, hots = [], []
            for tok in (t, t + 1):
                ids = ids_ref[pl.ds(m0 + tok, 1), :]
                gates.append(jnp.where(sub == (ids >> 7), g_ref[pl.ds(m0 + tok, 1), :], 0.0))
                hots.append(jnp.where(sub == (ids & (PEER_NKEYS - 1)), 1.0, 0.0))
            lhs = jnp.concatenate([jnp.concatenate([gates[0], zeros], axis=1),
                                   jnp.concatenate([zeros, gates[1]], axis=1)], axis=0)
            bt = jnp.concatenate(hots, axis=1)
            r = lax.dot_general(lhs.astype(BF16), bt.astype(BF16), _NT, preferred_element_type=F32)
            r_ref[t] = r[:PEER_NKEYS]
            r_ref[t + 1] = r[PEER_NKEYS:]

    def emit(grp, r_ref):
        m0 = pl.multiple_of(grp * PEER_GROUP, PEER_GROUP)
        for gi in range(PEER_NKEYS // SUBLANES):
            rows = slice(gi * SUBLANES, (gi + 1) * SUBLANES)
            ya = _transpose_sublane_blocks([r_ref[t, rows, :] for t in range(SUBLANES)])
            yb = _transpose_sublane_blocks([r_ref[SUBLANES + t, rows, :] for t in range(SUBLANES)])
            for r in range(SUBLANES):
                tile = jnp.concatenate([ya[r], yb[r]], axis=0).astype(BF16)
                w_ref[gi * SUBLANES + r, pl.ds(m0, PEER_GROUP), :] = tile

    maps(0, ra_ref)

    def body(k, carry):
        g0 = 2 * k
        maps(g0 + 1, rb_ref)
        emit(g0, ra_ref)
        maps(g0 + 2, ra_ref)
        emit(g0 + 1, rb_ref)
        return carry

    lax.fori_loop(0, n_grp // 2 - 1, body, 0)
    maps(n_grp - 1, rb_ref)
    emit(n_grp - 2, ra_ref)
    emit(n_grp - 1, rb_ref)


def peer_onehot(ids_t, g_t, tm=256):
    m = ids_t.shape[1]
    assert tm % (2 * PEER_GROUP) == 0 and tm % LANES == 0
    res = pltpu.VMEM((PEER_GROUP, PEER_NKEYS, PEER_NKEYS), F32)
    return pl.pallas_call(
        _peer_onehot_kernel,
        grid=(m // tm,),
        in_specs=[pl.BlockSpec((PEER_SLOTS, tm), lambda i: (0, i))] * 2,
        out_specs=pl.BlockSpec((PEER_NKEYS, tm, PEER_NKEYS), lambda i: (0, i, 0)),
        out_shape=jax.ShapeDtypeStruct((PEER_NKEYS, m, PEER_NKEYS), BF16),
        scratch_shapes=[pltpu.VMEM((tm, PEER_SLOTS), jnp.int32), pltpu.VMEM((tm, PEER_SLOTS), F32), res, res],
        compiler_params=_params("parallel"),
        name="peer_onehot",
    )(ids_t, g_t)


def _peer_dense_kernel(h_ref, gain_ref, w_ref, u_ref, v_ref, fg_ref, o_ref, hn_ref, acc_ref, *,
                       groups, final_norm):
    j = pl.program_id(1)

    @pl.when(j == 0)
    def _():
        hn_ref[...] = _rms(h_ref[...], gain_ref[...]).astype(BF16)
        acc_ref[...] = jnp.zeros_like(acc_ref)

    z = jnp.dot(hn_ref[...], u_ref[...], preferred_element_type=F32)
    wsel = jnp.concatenate([w_ref[gi] for gi in range(groups)], axis=-1)
    act = _gelu_tanh(z).astype(BF16) * wsel
    acc_ref[...] += jnp.dot(act, v_ref[...], preferred_element_type=F32)

    @pl.when(j == pl.num_programs(1) - 1)
    def _():
        out = h_ref[...] + acc_ref[...]
        if final_norm:
            out = _rms(out, fg_ref[...])
        o_ref[...] = out


def peer_dense(h, gain, wmap, ut_all, v_all, layer, final_gain, final_norm, tm=512, groups=16):
    m, d = h.shape
    ne = v_all.shape[1]
    te = groups * PEER_NKEYS
    return pl.pallas_call(
        functools.partial(_peer_dense_kernel, groups=groups, final_norm=final_norm),
        grid=(m // tm, ne // te),
        in_specs=[pl.BlockSpec((tm, d), lambda i, j: (i, 0)),
                  pl.BlockSpec((1, d), lambda i, j: (0, 0)),
                  pl.BlockSpec((groups, tm, PEER_NKEYS), lambda i, j: (j, i, 0)),
                  pl.BlockSpec((None, d, te), lambda i, j: (layer, 0, j)),
                  pl.BlockSpec((None, te, d), lambda i, j: (layer, j, 0)),
                  pl.BlockSpec((1, d), lambda i, j: (0, 0))],
        out_specs=pl.BlockSpec((tm, d), lambda i, j: (i, 0)),
        out_shape=jax.ShapeDtypeStruct((m, d), F32),
        scratch_shapes=[pltpu.VMEM((tm, d), BF16), pltpu.VMEM((tm, d), F32)],
        compiler_params=_params("parallel", "arbitrary"),
        name="peer_dense",
    )(h, gain.reshape(1, d), wmap, ut_all, v_all, final_gain.reshape(1, d))


def peer_ffn(h, q, gain, sub_keys, ut_all, v_all, layer, final_gain, final_norm):
    ids, g = peer_topk(q, sub_keys)
    wmap = peer_onehot(ids, g)
    return peer_dense(h, gain, wmap, ut_all, v_all, layer, final_gain, final_norm)


def gla_dilated_layer(h, bsz, gain, w_in, w_alpha, b_alpha, head_gain, w_out, ffn_gain, w_query):
    m, d = h.shape
    t = m // bsz
    sizes = [GLA_QK_W, GLA_QK_W, GLA_V_W, GLA_V_W, GLA_GATE_RANK, ATT_W, ATT_W, ATT_W]
    offs = [0]
    for s in sizes:
        offs.append(offs[-1] + s)
    seg = lambda i: w_in[:, offs[i]:offs[i + 1]]
    pad = jnp.zeros((d, LANES - GLA_GATE_RANK), w_in.dtype)
    w_cat = jnp.concatenate([seg(0), seg(1), seg(2), seg(3), seg(5), seg(6), seg(7), seg(4), pad], axis=1)
    widths = [GLA_QK_W, GLA_QK_W, GLA_V_W, GLA_V_W, ATT_W, ATT_W, ATT_W, LANES]
    dts = [F32] * len(widths)
    gq, gk, gv, gg, aq, ak, av, glr = norm_matmul(h, gain, w_cat.astype(BF16), widths, dts,
                                                  lane_major=(4, 5, 6))
    r3 = lambda a: a.reshape(bsz, t, a.shape[1])
    o_gla = gla(r3(gq), r3(gk), r3(gv), r3(gg), r3(glr), w_alpha, b_alpha, head_gain)
    outs, lses = [], []
    for window, dil in DILATED_BRANCHES:
        o, l = dilated_branch(aq, ak, av, bsz, window, dil)
        outs.append(o)
        lses.append(l)
    return combine_out(o_gla.reshape(m, GLA_V_W), outs, lses, w_out.astype(BF16), h, ffn_gain,
                       w_query.astype(BF16))


def rglru_layer(h, bsz, gain, w_in, conv_w, conv_b, w_r, b_r, w_i, b_i, lam, w_out, ffn_gain, w_query):
    m, d = h.shape
    t = m // bsz
    d_rnn = w_in.shape[1] // 2
    gate, xr = norm_matmul(h, gain, w_in.astype(BF16), [d_rnn, d_rnn], [F32, F32])
    y = lru(gate.reshape(bsz, t, d_rnn), xr.reshape(bsz, t, d_rnn), conv_w, conv_b, w_r, b_r, w_i, b_i, lam)
    return matmul_res(y.reshape(m, d_rnn), w_out.astype(BF16), h, ffn_gain, w_query.astype(BF16))


def kernel(x, mix_norm, ffn_norm, ab_w_in, gla_w_alpha, gla_b_alpha, gla_head_gain, ab_w_out,
           c_w_in, c_conv_w, c_conv_b, c_w_rgate, c_b_rgate, c_w_igate, c_b_igate, c_lambda, c_w_out,
           peer_w_query, peer_sub_keys, peer_u, peer_v, final_norm):
    bsz, t, d = x.shape
    depth = mix_norm.shape[0]
    h = x.reshape(bsz * t, d)
    ut_all = peer_u.astype(BF16).transpose(0, 2, 1)
    v_all = peer_v.astype(BF16)
    for layer in range(depth):
        if layer % 2 == 0:
            e = layer // 2
            h, q = gla_dilated_layer(h, bsz, mix_norm[layer], ab_w_in[e], gla_w_alpha[e], gla_b_alpha[e],
                                     gla_head_gain[e], ab_w_out[e], ffn_norm[layer], peer_w_query[layer])
        else:
            o = layer // 2
            h, q = rglru_layer(h, bsz, mix_norm[layer], c_w_in[o], c_conv_w[o], c_conv_b[o], c_w_rgate[o],
                               c_b_rgate[o], c_w_igate[o], c_b_igate[o], c_lambda[o], c_w_out[o],
                               ffn_norm[layer], peer_w_query[layer])
        h = peer_ffn(h, q, ffn_norm[layer], peer_sub_keys[layer], ut_all, v_all, layer,
                     final_norm, final_norm=(layer == depth - 1))
    return h.reshape(bsz, t, d)
```

```python
import functools
import math

import jax
import jax.numpy as jnp
from jax import lax
from jax.experimental import pallas as pl
from jax.experimental.pallas import tpu as pltpu

F32 = jnp.float32
BF16 = jnp.bfloat16

NORM_EPS = 1e-6
V7X_VMEM_LIMIT = 56 * 1024 * 1024

GLA_HEADS = 4
GLA_DK = 64
GLA_DV = 128
GLA_GATE_RANK = 16
GLA_GATE_TAU = 16.0
GLA_CHUNK = 64
ATT_HEADS = 8
ATT_HD = 64
DILATED_BRANCHES = ((128, 1), (512, 4), (2048, 16))
GLA_QK_W = GLA_HEADS * GLA_DK
GLA_V_W = GLA_HEADS * GLA_DV
ATT_W = ATT_HEADS * ATT_HD
CONV_W = 4
LRU_C = 8.0
RNN_BLOCK_W = 128
PEER_HEADS = 8
PEER_NKEYS = 128
PEER_TOPK = 16
PEER_SLOTS = PEER_HEADS * PEER_TOPK
LANES = 128
SUBLANES = 8

_NT = (((1,), (1,)), ((), ()))
_TN = (((0,), (0,)), ((), ()))


def _params(*sem):
    return pltpu.CompilerParams(dimension_semantics=sem, vmem_limit_bytes=V7X_VMEM_LIMIT)


def _gelu_tanh(x):
    c = math.sqrt(2.0 / math.pi)
    return (0.5 * x) * (1.0 + jnp.tanh(x * (c + (0.044715 * c) * (x * x))))


def _rms(x, gain):
    ms = jnp.mean(x * x, axis=-1, keepdims=True)
    return x * lax.rsqrt(ms + NORM_EPS) * gain


def _norm_matmul_kernel(x_ref, g_ref, w_ref, *o_refs, widths):
    xn = _rms(x_ref[...], g_ref[...]).astype(BF16)
    off = 0
    for o_ref, w in zip(o_refs, widths):
        y = jnp.dot(xn, w_ref[:, off:off + w], preferred_element_type=F32).astype(o_ref.dtype)
        if len(o_ref.shape) == 3:
            for c in range(w // LANES):
                o_ref[c] = y[:, c * LANES:(c + 1) * LANES]
        else:
            o_ref[...] = y
        off += w


def norm_matmul(x, gain, w_bf16, widths, dtypes, tm=512, lane_major=()):
    m, d = x.shape
    n = w_bf16.shape[1]
    assert sum(widths) == n and m % tm == 0
    out_specs, out_shape = [], []
    for k, (w, dt) in enumerate(zip(widths, dtypes)):
        if k in lane_major:
            out_specs.append(pl.BlockSpec((w // LANES, tm, LANES), lambda i: (0, i, 0)))
            out_shape.append(jax.ShapeDtypeStruct((w // LANES, m, LANES), dt))
        else:
            out_specs.append(pl.BlockSpec((tm, w), lambda i: (i, 0)))
            out_shape.append(jax.ShapeDtypeStruct((m, w), dt))
    return pl.pallas_call(
        functools.partial(_norm_matmul_kernel, widths=tuple(widths)),
        grid=(m // tm,),
        in_specs=[pl.BlockSpec((tm, d), lambda i: (i, 0)),
                  pl.BlockSpec((1, d), lambda i: (0, 0)),
                  pl.BlockSpec((d, n), lambda i: (0, 0))],
        out_specs=out_specs,
        out_shape=out_shape,
        compiler_params=_params("parallel"),
        name="norm_matmul",
    )(x, gain.reshape(1, d), w_bf16)


def _query_epilogue(h_new, g2_ref, wq_ref, q_ref):
    hn = _rms(h_new, g2_ref[...]).astype(BF16)
    q_ref[...] = jnp.dot(hn, wq_ref[...], preferred_element_type=F32)


def _matmul_res_kernel(x_ref, w_ref, r_ref, g2_ref, wq_ref, o_ref, q_ref):
    h_new = r_ref[...] + jnp.dot(x_ref[...].astype(BF16), w_ref[...], preferred_element_type=F32)
    o_ref[...] = h_new
    _query_epilogue(h_new, g2_ref, wq_ref, q_ref)


def matmul_res(x, w_bf16, res, gain2, wq_bf16, tm=512):
    m, k = x.shape
    n = w_bf16.shape[1]
    nq = wq_bf16.shape[1]
    return pl.pallas_call(
        _matmul_res_kernel,
        grid=(m // tm,),
        in_specs=[pl.BlockSpec((tm, k), lambda i: (i, 0)),
                  pl.BlockSpec((k, n), lambda i: (0, 0)),
                  pl.BlockSpec((tm, n), lambda i: (i, 0)),
                  pl.BlockSpec((1, n), lambda i: (0, 0)),
                  pl.BlockSpec((n, nq), lambda i: (0, 0))],
        out_specs=[pl.BlockSpec((tm, n), lambda i: (i, 0)), pl.BlockSpec((tm, nq), lambda i: (i, 0))],
        out_shape=[jax.ShapeDtypeStruct((m, n), F32), jax.ShapeDtypeStruct((m, nq), F32)],
        compiler_params=_params("parallel"),
        name="matmul_res",
    )(x, w_bf16, res, gain2.reshape(1, n), wq_bf16)


def _gla_kernel(q_ref, k_ref, v_ref, gg_ref, lr_ref, wa_ref, ba_ref, hg_ref, o_ref, st_ref, *, tg):
    c = GLA_CHUNK

    @pl.when(pl.program_id(1) == 0)
    def _():
        st_ref[...] = jnp.zeros_like(st_ref)

    z = jnp.dot(lr_ref[0], wa_ref[...], preferred_element_type=F32,
                precision=lax.Precision.HIGHEST) + ba_ref[...]
    log_a = (jnp.minimum(z, 0.0) - jnp.log1p(jnp.exp(-jnp.abs(z)))) * (1.0 / GLA_GATE_TAU)
    nh = GLA_HEADS
    tril = (lax.broadcasted_iota(jnp.int32, (c, c), 0) >= lax.broadcasted_iota(jnp.int32, (c, c), 1)).astype(F32)
    klane_head = lax.broadcasted_iota(jnp.int32, (1, GLA_QK_W), 1) // GLA_DK
    vlane_head = lax.broadcasted_iota(jnp.int32, (1, GLA_V_W), 1) // GLA_DV
    row_head = lax.broadcasted_iota(jnp.int32, (nh * c, 1), 0) // c
    row_t = lax.broadcasted_iota(jnp.int32, (nh * c, c), 0) % c
    causal_rows = row_t >= lax.broadcasted_iota(jnp.int32, (nh * c, c), 1)
    q_rows_mask = row_head == klane_head
    o_rows_mask = row_head == vlane_head
    st_mask = (lax.broadcasted_iota(jnp.int32, (GLA_V_W, 1), 0) // GLA_DV) == klane_head
    for ci in range(tg // c):
        sl = slice(ci * c, (ci + 1) * c)
        b = jnp.dot(tril, log_a[sl], preferred_element_type=F32, precision=lax.Precision.HIGHEST)
        b_last = b[c - 1:c, :]
        q_dec = q_ref[0, sl, :] * (GLA_DK ** -0.5) * jnp.exp(b)
        kk = k_ref[0, sl, :]
        k_inv = (kk * jnp.exp(-b)).astype(BF16)
        k_end = (kk * jnp.exp(b_last - b)).astype(BF16)
        decay = jnp.exp(b_last)
        vv = v_ref[0, sl, :].astype(BF16)
        qb = q_dec.astype(BF16)
        q_stack = jnp.where(q_rows_mask, jnp.concatenate([qb] * nh, axis=0), jnp.zeros((), BF16))
        att = lax.dot_general(q_stack, k_inv, _NT, preferred_element_type=F32)
        att = jnp.where(causal_rows, att, 0.0).astype(BF16)
        o_all = jnp.dot(att, vv, preferred_element_type=F32)
        o_all = jnp.where(o_rows_mask, o_all, 0.0)
        o = o_all[0:c]
        for h in range(1, nh):
            o = o + o_all[h * c:(h + 1) * c]
        st = st_ref[...]
        o = o + lax.dot_general(qb, st.astype(BF16), _NT, preferred_element_type=F32)
        kv_t = lax.dot_general(vv, k_end, _TN, preferred_element_type=F32)
        st_ref[...] = decay * st + jnp.where(st_mask, kv_t, 0.0)
        outs = []
        for h in range(nh):
            o_h = o[:, h * GLA_DV:(h + 1) * GLA_DV]
            outs.append(o_h * lax.rsqrt(jnp.mean(o_h * o_h, axis=-1, keepdims=True) + NORM_EPS))
        o = jnp.concatenate(outs, axis=-1)
        gg = gg_ref[0, sl, :]
        o_ref[0, sl, :] = o * hg_ref[...] * (gg * jax.nn.sigmoid(gg))


def gla(gq, gk, gv, gg, glr, w_alpha, b_alpha, head_gain, tg=256):
    bsz, t, _ = gq.shape
    wa = jnp.zeros((LANES, GLA_QK_W), F32).at[:GLA_GATE_RANK].set(w_alpha)
    spec = lambda w: pl.BlockSpec((1, tg, w), lambda b, i: (b, i, 0))
    const = lambda r, w: pl.BlockSpec((r, w), lambda b, i: (0, 0))
    return pl.pallas_call(
        functools.partial(_gla_kernel, tg=tg),
        grid=(bsz, t // tg),
        in_specs=[spec(GLA_QK_W), spec(GLA_QK_W), spec(GLA_V_W), spec(GLA_V_W), spec(LANES),
                  const(LANES, GLA_QK_W), const(1, GLA_QK_W), const(1, GLA_V_W)],
        out_specs=spec(GLA_V_W),
        out_shape=jax.ShapeDtypeStruct((bsz, t, GLA_V_W), F32),
        scratch_shapes=[pltpu.VMEM((GLA_V_W, GLA_QK_W), F32)],
        compiler_params=_params("parallel", "arbitrary"),
        name="gla",
    )(gq, gk, gv, gg, glr, wa, b_alpha.reshape(1, -1), head_gain.reshape(1, -1))


def _dilated_kernel(q_ref, kp_ref, kc_ref, vp_ref, vc_ref, o_ref, l_ref, *, dil, blk):
    i = pl.program_id(1)
    cblk = pl.program_id(2)
    n_pairs = q_ref.shape[0]
    n_cblk = ATT_HEADS // 2 // n_pairs
    n_sub = q_ref.shape[1] // (blk * dil)
    qi = lax.broadcasted_iota(jnp.int32, (blk, 2 * blk), 0)
    kj = lax.broadcasted_iota(jnp.int32, (blk, 2 * blk), 1)
    delta = qi + blk - kj
    band = (delta >= 0) & (delta <= blk)
    valid = [band & (kj >= jnp.where(i > 0, 0, blk))] + [band] * (n_sub - 1)
    dist = (delta * dil).astype(F32)
    lane = lax.broadcasted_iota(jnp.int32, (1, 2 * ATT_HD), 1)
    lo = lane < ATT_HD

    def alibi(lp, e, ok):
        slopes = [2.0 ** (-8.0 * (2 * (c * n_pairs + lp) + e + 1) / ATT_HEADS) for c in range(n_cblk)]
        slope = jnp.float32(slopes[-1])
        for c in range(n_cblk - 2, -1, -1):
            slope = jnp.where(cblk == c, jnp.float32(slopes[c]), slope)
        return jnp.where(ok, -slope * dist, -1e30)

    biases = [[[alibi(lp, e, ok) for e in range(2)] for lp in range(n_pairs)] for ok in valid[:2]]

    per_trip = max(1, min(dil, 2 * ATT_HEADS // (2 * n_pairs * n_sub)))

    def body(trip, carry):
        for sub in range(per_trip):
            one_class(trip + sub * (dil // per_trip))
        return carry

    def one_class(r):
        for sb in range(n_sub):
            rows = pl.ds(r + sb * blk * dil, blk, stride=dil)
            if sb == 0:
                p_ref_k, p_ref_v, prows = kp_ref, vp_ref, pl.ds(r + (n_sub - 1) * blk * dil, blk, stride=dil)
            else:
                p_ref_k, p_ref_v, prows = kc_ref, vc_ref, pl.ds(r + (sb - 1) * blk * dil, blk, stride=dil)
            bias = biases[min(sb, 1)]
            for lp in range(n_pairs):
                q2 = (q_ref[lp, rows, :] * (ATT_HD ** -0.5)).astype(BF16)
                k2 = jnp.concatenate([p_ref_k[lp, prows, :], kc_ref[lp, rows, :]], axis=0).astype(BF16)
                v2 = jnp.concatenate([p_ref_v[lp, prows, :], vc_ref[lp, rows, :]], axis=0).astype(BF16)
                o_pair = jnp.zeros((blk, 2 * ATT_HD), F32)
                lse_pair = jnp.zeros((blk, 2 * ATT_HD), F32)
                for e in range(2):
                    sel = lo if e == 0 else jnp.logical_not(lo)
                    qm = jnp.where(sel, q2, jnp.zeros_like(q2))
                    vm = jnp.where(sel, v2, jnp.zeros_like(v2))
                    s = lax.dot_general(qm, k2, _NT, preferred_element_type=F32) + bias[lp][e]
                    m = jnp.max(s, axis=-1, keepdims=True)
                    pexp = jnp.exp(s - m)
                    l = jnp.sum(pexp, axis=-1, keepdims=True)
                    o_e = jnp.dot(pexp.astype(BF16), vm, preferred_element_type=F32)
                    o_pair = o_pair + o_e * (1.0 / l)
                    lse_pair = jnp.where(sel, m + jnp.log(l), lse_pair)
                o_ref[lp, rows, :] = o_pair
                l_ref[lp, rows, :] = lse_pair

    if dil == per_trip:
        body(0, 0)
    else:
        lax.fori_loop(0, dil // per_trip, body, 0)


def dilated_branch(q, k, v, bsz, window, dil):
    n_slab, m, pw = q.shape
    t = m // bsz
    blk = window // dil
    rows = blk * dil * (2 if blk * dil <= LANES else 1)
    assert t % rows == 0 and pw == 2 * ATT_HD
    nblk = t // rows
    n_pairs = n_slab if rows <= 512 else n_slab // 2
    cur = pl.BlockSpec((n_pairs, rows, pw), lambda b, i, c: (c, b * nblk + i, 0))
    prev = pl.BlockSpec((n_pairs, rows, pw), lambda b, i, c: (c, b * nblk + jnp.maximum(i - 1, 0), 0))
    return pl.pallas_call(
        functools.partial(_dilated_kernel, dil=dil, blk=blk),
        grid=(bsz, nblk, n_slab // n_pairs),
        in_specs=[cur, prev, cur, prev, cur],
        out_specs=[cur, cur],
        out_shape=[jax.ShapeDtypeStruct((n_slab, m, pw), F32)] * 2,
        compiler_params=_params("parallel", "arbitrary", "arbitrary"),
        name=f"dilated_d{dil}",
    )(q, k, k, v, v)


def _combine_out_kernel(og_ref, o1_ref, o2_ref, o3_ref, l1_ref, l2_ref, l3_ref, w_ref, r_ref, g2_ref, wq_ref,
                        out_ref, q_ref):
    wide = lambda ref: jnp.concatenate([ref[c] for c in range(ref.shape[0])], axis=-1)
    l1, l2, l3 = wide(l1_ref), wide(l2_ref), wide(l3_ref)
    mx = jnp.maximum(jnp.maximum(l1, l2), l3)
    e1, e2, e3 = jnp.exp(l1 - mx), jnp.exp(l2 - mx), jnp.exp(l3 - mx)
    att = (e1 * wide(o1_ref) + e2 * wide(o2_ref) + e3 * wide(o3_ref)) / (e1 + e2 + e3)
    kg = og_ref.shape[1]
    acc = jnp.dot(og_ref[...].astype(BF16), w_ref[:kg, :], preferred_element_type=F32)
    acc = acc + jnp.dot(att.astype(BF16), w_ref[kg:, :], preferred_element_type=F32)
    h_new = r_ref[...] + acc
    out_ref[...] = h_new
    _query_epilogue(h_new, g2_ref, wq_ref, q_ref)


def combine_out(o_gla, outs, lses, w_bf16, res, gain2, wq_bf16, tm=512):
    m = res.shape[0]
    n = w_bf16.shape[1]
    nq = wq_bf16.shape[1]
    row = lambda w: pl.BlockSpec((tm, w), lambda i: (i, 0))
    slab = pl.BlockSpec((ATT_W // LANES, tm, LANES), lambda i: (0, i, 0))
    return pl.pallas_call(
        _combine_out_kernel,
        grid=(m // tm,),
        in_specs=[row(GLA_V_W)] + [slab] * 6
                 + [pl.BlockSpec(w_bf16.shape, lambda i: (0, 0)), row(n),
                    pl.BlockSpec((1, n), lambda i: (0, 0)), pl.BlockSpec((n, nq), lambda i: (0, 0))],
        out_specs=[row(n), row(nq)],
        out_shape=[jax.ShapeDtypeStruct((m, n), F32), jax.ShapeDtypeStruct((m, nq), F32)],
        compiler_params=_params("parallel"),
        name="combine_out",
    )(o_gla, *outs, *lses, w_bf16, res, gain2.reshape(1, n), wq_bf16)


def _lru_kernel(gate_ref, xr_ref, cw_ref, cb_ref, wr_ref, br_ref, wi_ref, bi_ref, lam_ref, y_ref,
                ext_ref, h_ref, *, tt):
    @pl.when(pl.program_id(1) == 0)
    def _():
        ext_ref[:SUBLANES, :] = jnp.zeros((SUBLANES, ext_ref.shape[1]), F32)
        h_ref[...] = jnp.zeros_like(h_ref)

    xr = xr_ref[0]
    d = xr.shape[1]
    ext_ref[SUBLANES:, :] = xr
    xc = cb_ref[...] + cw_ref[CONV_W - 1:CONV_W, :] * xr
    for j in range(CONV_W - 1):
        off = SUBLANES - (CONV_W - 1) + j
        xc = xc + cw_ref[j:j + 1, :] * ext_ref[off:off + tt, :]
    ext_ref[:SUBLANES, :] = xr[tt - SUBLANES:, :]
    xcb = xc.astype(BF16)
    rs, is_ = [], []
    for g in range(d // RNN_BLOCK_W):
        cs = slice(g * RNN_BLOCK_W, (g + 1) * RNN_BLOCK_W)
        rs.append(jnp.dot(xcb[:, cs], wr_ref[g], preferred_element_type=F32))
        is_.append(jnp.dot(xcb[:, cs], wi_ref[g], preferred_element_type=F32))
    r = jax.nn.sigmoid(jnp.concatenate(rs, axis=-1) + br_ref[...])
    ig = jax.nn.sigmoid(jnp.concatenate(is_, axis=-1) + bi_ref[...])
    lam = lam_ref[...]
    softplus = jnp.maximum(-lam, 0.0) + jnp.log1p(jnp.exp(-jnp.abs(lam)))
    log_a = (-LRU_C) * r * softplus
    a = jnp.exp(log_a)
    th = jnp.tanh(log_a)
    n = -2.0 * th
    u = jnp.where(n > 0.0, n * lax.rsqrt(n), 0.0) * lax.rsqrt(1.0 - th) * (ig * xc)
    a = a.reshape(tt // SUBLANES, SUBLANES, d)
    u = u.reshape(tt // SUBLANES, SUBLANES, d)
    rows = lax.broadcasted_iota(jnp.int32, (1, SUBLANES, 1), 1)
    s = 1
    while s < SUBLANES:
        keep = rows >= s
        a_sh = jnp.where(keep, pltpu.roll(a, s, axis=1), 1.0)
        u_sh = jnp.where(keep, pltpu.roll(u, s, axis=1), 0.0)
        u = u + a * u_sh
        a = a * a_sh
        s *= 2
    carry = h_ref[...]
    gate = gate_ref[0]
    for g in range(tt // SUBLANES):
        hs = u[g] + a[g] * carry
        carry = hs[SUBLANES - 1:SUBLANES, :]
        y_ref[0, g * SUBLANES:(g + 1) * SUBLANES, :] = _gelu_tanh(gate[g * SUBLANES:(g + 1) * SUBLANES]) * hs
    h_ref[...] = carry


def lru(gate, xr, conv_w, conv_b, w_r, b_r, w_i, b_i, lam, tt=256):
    bsz, t, d = xr.shape
    spec = pl.BlockSpec((1, tt, d), lambda b, i: (b, i, 0))
    row = pl.BlockSpec((1, d), lambda b, i: (0, 0))
    wspec = pl.BlockSpec(w_r.shape, lambda b, i: (0, 0, 0))
    return pl.pallas_call(
        functools.partial(_lru_kernel, tt=tt),
        grid=(bsz, t // tt),
        in_specs=[spec, spec, pl.BlockSpec((CONV_W, d), lambda b, i: (0, 0)), row,
                  wspec, row, wspec, row, row],
        out_specs=spec,
        out_shape=jax.ShapeDtypeStruct((bsz, t, d), F32),
        scratch_shapes=[pltpu.VMEM((tt + SUBLANES, d), F32), pltpu.VMEM((1, d), F32)],
        compiler_params=_params("parallel", "arbitrary"),
        name="lru",
    )(gate, xr, conv_w, conv_b.reshape(1, d), w_r.astype(BF16), b_r.reshape(1, d),
      w_i.astype(BF16), b_i.reshape(1, d), lam.reshape(1, d))


def _extract_top(vals, pos, n_rounds, payload=None):
    return _extract_top_many([(vals, pos, payload)], n_rounds)[0]


def _extract_top_many(problems, n_rounds):
    state, outs = [], []
    for vals, pos, payload in problems:
        lo_wins = vals[0] >= vals[1]
        win = lambda x, lo_wins=lo_wins: jnp.where(lo_wins, x[0], x[1])
        lose = lambda x, lo_wins=lo_wins: jnp.where(lo_wins, x[1], x[0])
        st = dict(v_w=win(vals), v_l=lose(vals), p_w=win(pos), p_l=lose(pos))
        if payload is not None:
            st.update(x_w=win(payload), x_l=lose(payload))
        state.append(st)
        outs.append(([], [], []))
    for _ in range(n_rounds):
        for st, (out_v, out_p, out_x) in zip(state, outs):
            m = jnp.max(st["v_w"], axis=0, keepdims=True)
            p = jnp.min(jnp.where(st["v_w"] == m, st["p_w"], 1e9), axis=0, keepdims=True)
            first = st["p_w"] == p
            if "x_w" in st:
                out_x.append(jnp.max(jnp.where(first, st["x_w"], -1.0), axis=0, keepdims=True))
                st["x_w"] = jnp.where(first, st["x_l"], st["x_w"])
            st["v_w"] = jnp.where(first, st["v_l"], st["v_w"])
            st["p_w"] = jnp.where(first, st["p_l"], st["p_w"])
            st["v_l"] = jnp.where(first, -jnp.inf, st["v_l"])
            out_v.append(m)
            out_p.append(p)
    return outs


def _topk_heads(q_ref, keys_ref):
    k = PEER_TOPK
    tm = q_ref.shape[0]
    n_heads = keys_ref.shape[0]
    rows = lambda n: lax.broadcasted_iota(jnp.int32, (n, tm), 0).astype(F32)
    half = PEER_NKEYS // 2
    key_pos = (rows(half), rows(half) + half)
    halves = ([(0, 16), (1, 8), (2, 8), (3, 8)], [(4, 8), (5, 8), (6, 8), (7, 8)])
    id_rows, g_rows = [], []
    for h in range(n_heads):
        scores = []
        for p in range(2):
            keys = keys_ref[h, p]
            qs = q_ref[:, (2 * h + p) * LANES:(2 * h + p + 1) * LANES]
            s_t = tuple(lax.dot_general(keys[kh * half:(kh + 1) * half], qs, _NT, preferred_element_type=F32,
                                        precision=lax.Precision.HIGHEST) for kh in range(2))
            scores.append((s_t, key_pos, None))
        tops = [(jnp.concatenate(v, axis=0), jnp.concatenate(pidx, axis=0))
                for v, pidx, _ in _extract_top_many(scores, k)]
        (t1, i1), (t2, i2) = tops
        cand_v, cand_i, cand_p = [], [], []
        for hi, groups in enumerate(halves):
            cv = [t1[a:a + 1, :] + t2[0:nb, :] for a, nb in groups]
            ci = [i1[a:a + 1, :] * PEER_NKEYS + i2[0:nb, :] for a, nb in groups]
            cp = [a * k + rows(nb) for a, nb in groups]
            if hi == 1:
                cv.append(t1[8:16, :] + t2[0:1, :])
                ci.append(i1[8:16, :] * PEER_NKEYS + i2[0:1, :])
                cp.append((8.0 + rows(8)) * k)
            cand_v.append(jnp.concatenate(cv, axis=0))
            cand_i.append(jnp.concatenate(ci, axis=0))
            cand_p.append(jnp.concatenate(cp, axis=0))
        bv, _, bi = _extract_top(cand_v, cand_p, k, payload=cand_i)
        best = jnp.concatenate(bv, axis=0)
        e = jnp.exp(best - best[0:1, :])
        g_rows.append(e / jnp.sum(e, axis=0, keepdims=True))
        id_rows.append(jnp.concatenate(bi, axis=0))
    return jnp.concatenate(id_rows, axis=0), jnp.concatenate(g_rows, axis=0)


def _peer_topk_kernel(q_ref, keys_ref, ids_ref, g_ref):
    ids_t, g_t = _topk_heads(q_ref, keys_ref)
    ids_ref[...] = ids_t.astype(jnp.int32)
    g_ref[...] = g_t


def peer_topk(q, sub_keys, tm=256):
    m = q.shape[0]
    return pl.pallas_call(
        _peer_topk_kernel,
        grid=(m // tm,),
        in_specs=[pl.BlockSpec((tm, q.shape[1]), lambda i: (i, 0)),
                  pl.BlockSpec(sub_keys.shape, lambda i: (0, 0, 0, 0))],
        out_specs=[pl.BlockSpec((PEER_SLOTS, tm), lambda i: (0, i))] * 2,
        out_shape=[jax.ShapeDtypeStruct((PEER_SLOTS, m), jnp.int32),
                   jax.ShapeDtypeStruct((PEER_SLOTS, m), F32)],
        compiler_params=_params("parallel"),
        name="peer_topk",
    )(q, sub_keys)


PEER_GROUP = 16


def _peer_onehot_kernel(ids_t_ref, g_t_ref, w_ref, ids_ref, g_ref, ra_ref, rb_ref):
    tm = ids_ref.shape[0]
    n_grp = tm // PEER_GROUP
    for c in range(tm // LANES):
        cs = slice(c * LANES, (c + 1) * LANES)
        ids_ref[cs, :] = ids_t_ref[:, cs].astype(F32).T.astype(jnp.int32)
        g_ref[cs, :] = g_t_ref[:, cs].T
    sub = lax.broadcasted_iota(jnp.int32, (PEER_NKEYS, PEER_SLOTS), 0)
    zeros = jnp.zeros((PEER_NKEYS, PEER_SLOTS), F32)

    def maps(grp, r_ref):
        m0 = pl.multiple_of(grp * PEER_GROUP, PEER_GROUP)
        for t in range(0, PEER_GROUP, 2):
            gates, hots = [], []
            for tok in (t, t + 1):
                ids = ids_ref[pl.ds(m0 + tok, 1), :]
                gates.append(jnp.where(sub == (ids >> 7), g_ref[pl.ds(m0 + tok, 1), :], 0.0))
                hots.append(jnp.where(sub == (ids & (PEER_NKEYS - 1)), 1.0, 0.0))
            lhs = jnp.concatenate([jnp.concatenate([gates[0], zeros], axis=1),
                                   jnp.concatenate([zeros, gates[1]], axis=1)], axis=0)
            bt = jnp.concatenate(hots, axis=1)
            r = lax.dot_general(lhs.astype(BF16), bt.astype(BF16), _NT, preferred_element_type=F32)
            r_ref[t] = r[:PEER_NKEYS]
            r_ref[t + 1] = r[PEER_NKEYS:]

    def emit(grp, r_ref):
        m0 = pl.multiple_of(grp * PEER_GROUP, PEER_GROUP)
        y = jnp.swapaxes(r_ref[...], 0, 1)
        w_ref[:, pl.ds(m0, PEER_GROUP), :] = y.astype(BF16)

    maps(0, ra_ref)

    def body(k, carry):
        g0 = 2 * k
        maps(g0 + 1, rb_ref)
        emit(g0, ra_ref)
        maps(g0 + 2, ra_ref)
        emit(g0 + 1, rb_ref)
        return carry

    lax.fori_loop(0, n_grp // 2 - 1, body, 0)
    maps(n_grp - 1, rb_ref)
    emit(n_grp - 2, ra_ref)
    emit(n_grp - 1, rb_ref)


def peer_onehot(ids_t, g_t, tm=256):
    m = ids_t.shape[1]
    assert tm % (2 * PEER_GROUP) == 0 and tm % LANES == 0
    res = pltpu.VMEM((PEER_GROUP, PEER_NKEYS, PEER_NKEYS), F32)
    return pl.pallas_call(
        _peer_onehot_kernel,
        grid=(m // tm,),
        in_specs=[pl.BlockSpec((PEER_SLOTS, tm), lambda i: (0, i))] * 2,
        out_specs=pl.BlockSpec((PEER_NKEYS, tm, PEER_NKEYS), lambda i: (0, i, 0)),
        out_shape=jax.ShapeDtypeStruct((PEER_NKEYS, m, PEER_NKEYS), BF16),
        scratch_shapes=[pltpu.VMEM((tm, PEER_SLOTS), jnp.int32), pltpu.VMEM((tm, PEER_SLOTS), F32), res, res],
        compiler_params=_params("parallel"),
        name="peer_onehot",
    )(ids_t, g_t)


def _peer_dense_kernel(h_ref, gain_ref, w_ref, u_ref, v_ref, fg_ref, o_ref, hn_ref, acc_ref, *,
                       groups, final_norm):
    j = pl.program_id(1)

    @pl.when(j == 0)
    def _():
        hn_ref[...] = _rms(h_ref[...], gain_ref[...]).astype(BF16)
        acc_ref[...] = jnp.zeros_like(acc_ref)

    z = jnp.dot(hn_ref[...], u_ref[...], preferred_element_type=F32)
    wsel = jnp.concatenate([w_ref[gi] for gi in range(groups)], axis=-1)
    act = _gelu_tanh(z).astype(BF16) * wsel
    acc_ref[...] += jnp.dot(act, v_ref[...], preferred_element_type=F32)

    @pl.when(j == pl.num_programs(1) - 1)
    def _():
        out = h_ref[...] + acc_ref[...]
        if final_norm:
            out = _rms(out, fg_ref[...])
        o_ref[...] = out


def peer_dense(h, gain, wmap, ut_all, v_all, layer, final_gain, final_norm, tm=512, groups=16):
    m, d = h.shape
    ne = v_all.shape[1]
    te = groups * PEER_NKEYS
    return pl.pallas_call(
        functools.partial(_peer_dense_kernel, groups=groups, final_norm=final_norm),
        grid=(m // tm, ne // te),
        in_specs=[pl.BlockSpec((tm, d), lambda i, j: (i, 0)),
                  pl.BlockSpec((1, d), lambda i, j: (0, 0)),
                  pl.BlockSpec((groups, tm, PEER_NKEYS), lambda i, j: (j, i, 0)),
                  pl.BlockSpec((None, d, te), lambda i, j: (layer, 0, j)),
                  pl.BlockSpec((None, te, d), lambda i, j: (layer, j, 0)),
                  pl.BlockSpec((1, d), lambda i, j: (0, 0))],
        out_specs=pl.BlockSpec((tm, d), lambda i, j: (i, 0)),
        out_shape=jax.ShapeDtypeStruct((m, d), F32),
        scratch_shapes=[pltpu.VMEM((tm, d), BF16), pltpu.VMEM((tm, d), F32)],
        compiler_params=_params("parallel", "arbitrary"),
        name="peer_dense",
    )(h, gain.reshape(1, d), wmap, ut_all, v_all, final_gain.reshape(1, d))


def peer_ffn(h, q, gain, sub_keys, ut_all, v_all, layer, final_gain, final_norm):
    ids, g = peer_topk(q, sub_keys)
    wmap = peer_onehot(ids, g)
    return peer_dense(h, gain, wmap, ut_all, v_all, layer, final_gain, final_norm)


def gla_dilated_layer(h, bsz, gain, w_in, w_alpha, b_alpha, head_gain, w_out, ffn_gain, w_query):
    m, d = h.shape
    t = m // bsz
    sizes = [GLA_QK_W, GLA_QK_W, GLA_V_W, GLA_V_W, GLA_GATE_RANK, ATT_W, ATT_W, ATT_W]
    offs = [0]
    for s in sizes:
        offs.append(offs[-1] + s)
    seg = lambda i: w_in[:, offs[i]:offs[i + 1]]
    pad = jnp.zeros((d, LANES - GLA_GATE_RANK), w_in.dtype)
    w_cat = jnp.concatenate([seg(0), seg(1), seg(2), seg(3), seg(5), seg(6), seg(7), seg(4), pad], axis=1)
    widths = [GLA_QK_W, GLA_QK_W, GLA_V_W, GLA_V_W, ATT_W, ATT_W, ATT_W, LANES]
    dts = [F32] * len(widths)
    gq, gk, gv, gg, aq, ak, av, glr = norm_matmul(h, gain, w_cat.astype(BF16), widths, dts,
                                                  lane_major=(4, 5, 6))
    r3 = lambda a: a.reshape(bsz, t, a.shape[1])
    o_gla = gla(r3(gq), r3(gk), r3(gv), r3(gg), r3(glr), w_alpha, b_alpha, head_gain)
    outs, lses = [], []
    for window, dil in DILATED_BRANCHES:
        o, l = dilated_branch(aq, ak, av, bsz, window, dil)
        outs.append(o)
        lses.append(l)
    return combine_out(o_gla.reshape(m, GLA_V_W), outs, lses, w_out.astype(BF16), h, ffn_gain,
                       w_query.astype(BF16))


def rglru_layer(h, bsz, gain, w_in, conv_w, conv_b, w_r, b_r, w_i, b_i, lam, w_out, ffn_gain, w_query):
    m, d = h.shape
    t = m // bsz
    d_rnn = w_in.shape[1] // 2
    gate, xr = norm_matmul(h, gain, w_in.astype(BF16), [d_rnn, d_rnn], [F32, F32])
    y = lru(gate.reshape(bsz, t, d_rnn), xr.reshape(bsz, t, d_rnn), conv_w, conv_b, w_r, b_r, w_i, b_i, lam)
    return matmul_res(y.reshape(m, d_rnn), w_out.astype(BF16), h, ffn_gain, w_query.astype(BF16))


def kernel(x, mix_norm, ffn_norm, ab_w_in, gla_w_alpha, gla_b_alpha, gla_head_gain, ab_w_out,
           c_w_in, c_conv_w, c_conv_b, c_w_rgate, c_b_rgate, c_w_igate, c_b_igate, c_lambda, c_w_out,
           peer_w_query, peer_sub_keys, peer_u, peer_v, final_norm):
    bsz, t, d = x.shape
    depth = mix_norm.shape[0]
    h = x.reshape(bsz * t, d)
    ut_all = peer_u.astype(BF16).transpose(0, 2, 1)
    v_all = peer_v.astype(BF16)
    for layer in range(depth):
        if layer % 2 == 0:
            e = layer // 2
            h, q = gla_dilated_layer(h, bsz, mix_norm[layer], ab_w_in[e], gla_w_alpha[e], gla_b_alpha[e],
                                     gla_head_gain[e], ab_w_out[e], ffn_norm[layer], peer_w_query[layer])
        else:
            o = layer // 2
            h, q = rglru_layer(h, bsz, mix_norm[layer], c_w_in[o], c_conv_w[o], c_conv_b[o], c_w_rgate[o],
                               c_b_rgate[o], c_w_igate[o], c_b_igate[o], c_lambda[o], c_w_out[o],
                               ffn_norm[layer], peer_w_query[layer])
        h = peer_ffn(h, q, ffn_norm[layer], peer_sub_keys[layer], ut_all, v_all, layer,
                     final_norm, final_norm=(layer == depth - 1))
    return h.reshape(bsz, t, d)
```
